```python
import math
import jax, jax.numpy as jnp
from jax import lax
import numpy as np

D_MODEL = 4096
BATCH = 4
SEQ = 4096
DEPTH = 4

CTX_LEN = 256
GRID_W = 64
ROPE_BASE = 10000.0
EPS = 1e-6

ADA_RANK = D_MODEL // 16
N_MOD = 9

D_FF = 11 * D_MODEL // 8

A_HEADS = D_MODEL // 256
A_QK = 64
A_V = 2 * A_QK
A_QK_W = A_HEADS * 2 * A_QK
A_WIDTH = A_HEADS * A_V
B_HEADS = D_MODEL // 512
B_QK = 128
B_V = 256
B_QK_W = B_HEADS * B_QK
B_WIDTH = B_HEADS * B_V
EVEN_WIDTHS = (A_QK_W, A_QK_W, A_WIDTH, B_QK_W, B_QK_W, B_WIDTH, B_WIDTH)
EVEN_IN = A_QK_W * 2 + A_WIDTH + B_QK_W * 2 + B_WIDTH * 2
MIX_WIDTH = A_WIDTH + B_WIDTH
Q_BLOCK = 128
RET_CHUNK = 128

C_HEADS = D_MODEL // 512
C_QK = 256
C_V = 512
C_QK_W = C_HEADS * C_QK
C_WIDTH = C_HEADS * C_V
GLA_RANK = 16
GLA_TAU = 16.0
GLA_CHUNK = 64
ODD_WIDTHS = (C_QK_W, C_QK_W, C_WIDTH, C_WIDTH, 2 * GLA_RANK)
ODD_IN = C_QK_W * 2 + C_WIDTH * 2 + 2 * GLA_RANK

kernel_name = "hybrid_diffattn_retention_gla_dit_trunk"


def rmsnorm(x, g):
    xf = x.astype(jnp.float32)
    y = xf * lax.rsqrt(jnp.mean(xf * xf, axis=-1, keepdims=True) + EPS)
    return (y * g.astype(jnp.float32)).astype(x.dtype)


def adaln(cvec, w_down, w_up, b_up):
    m = (jax.nn.silu(cvec) @ w_down) @ w_up + b_up
    return m.reshape(cvec.shape[0], N_MOD, -1)


def modulate(h, g, shift, scale):
    return rmsnorm(h, g) * (1.0 + scale[:, None, :]) + shift[:, None, :]


def swiglu(u, w_in, w_out):
    gate, up = jnp.split(u @ w_in, 2, axis=-1)
    return (jax.nn.silu(gate) * up) @ w_out


def ffn_sublayer(h, m, j, g_pre, g_post, w_in, w_out):
    u = modulate(h, g_pre, m[:, j], m[:, j + 1])
    return m[:, j + 2][:, None, :] * rmsnorm(swiglu(u, w_in, w_out), g_post)


def split_cols(p, widths):
    offs = np.cumsum(widths)[:-1].tolist()
    return jnp.split(p, offs, axis=-1)


def heads(t, n_heads):
    b, l, _ = t.shape
    return t.reshape(b, l, n_heads, -1).transpose(0, 2, 1, 3)


def merge(t):
    b, h, l, d = t.shape
    return t.transpose(0, 2, 1, 3).reshape(b, l, h * d)


def to_chunks(t, size):
    b, h, l, d = t.shape
    return jnp.moveaxis(t.reshape(b, h, l // size, size, d), 2, 0)


def from_chunks(t):
    n, b, h, s, d = t.shape
    return jnp.moveaxis(t, 0, 2).reshape(b, h, n * s, d)


def axial_rope_tables(rows, head_dim):
    row = jnp.repeat(jnp.arange(rows), GRID_W).astype(jnp.float32)
    col = jnp.tile(jnp.arange(GRID_W), rows).astype(jnp.float32)
    axis_dim = head_dim // 2
    inv_freq = ROPE_BASE ** (-jnp.arange(0, axis_dim, 2, dtype=jnp.float32) / axis_dim)
    ang_r = row[:, None] * inv_freq[None, :]
    ang_c = col[:, None] * inv_freq[None, :]
    return (jnp.cos(ang_r), jnp.sin(ang_r), jnp.cos(ang_c), jnp.sin(ang_c))


def _rot_half(t, cos, sin):
    t1, t2 = jnp.split(t, 2, axis=-1)
    return jnp.concatenate([t1 * cos - t2 * sin, t2 * cos + t1 * sin], axis=-1)


def rope_2d(t, tabs):
    cos_r, sin_r, cos_c, sin_c = tabs
    tr, tc = jnp.split(t, 2, axis=-1)
    out = jnp.concatenate([_rot_half(tr, cos_r, sin_r), _rot_half(tc, cos_c, sin_c)], axis=-1)
    return out.astype(t.dtype)


def diff_attend(q, k, v, lam):
    s = jnp.einsum('bhmqd,bhmkd->bhmqk', q, k).astype(jnp.float32) * (A_QK ** -0.5)
    p = jax.nn.softmax(s, axis=-1)
    a = p[:, :, 0] - lam * p[:, :, 1]
    return jnp.einsum('bhqk,bhkd->bhqd', a.astype(v.dtype), v)


def retention_scan(q, k, v, log_g, s0):
    C = RET_CHUNK
    idx = jnp.arange(C, dtype=jnp.float32)
    diff = idx[:, None] - idx[None, :]
    decay_intra = jnp.where(diff >= 0, jnp.exp(log_g[:, None, None] * jnp.maximum(diff, 0.0)), 0.0)
    q_dec = jnp.exp(log_g[:, None] * (idx + 1.0))[:, :, None]
    k_dec = jnp.exp(log_g[:, None] * (C - 1.0 - idx))[:, :, None]
    chunk_dec = jnp.exp(log_g * C)[:, None, None]

    def step(s, inp):
        qc, kc, vc = inp
        inner = jnp.einsum('bhid,bhjd->bhij', qc, kc) * decay_intra
        o = jnp.einsum('bhij,bhjv->bhiv', inner, vc) + jnp.einsum('bhid,bhdv->bhiv', qc * q_dec, s)
        s = s * chunk_dec + jnp.einsum('bhjd,bhjv->bhdv', kc * k_dec, vc)
        return s, o

    f32 = lambda t: to_chunks(t.astype(jnp.float32), C)
    s, o = lax.scan(step, s0, (f32(q), f32(k), f32(v)))
    return from_chunks(o).astype(v.dtype), s


def gla_scan(q, k, v, log_a, s0):
    C = GLA_CHUNK
    tri = jnp.tril(jnp.ones((C, C), dtype=bool))[:, :, None]

    def step(s, inp):
        qc, kc, vc, ac = inp
        b = jnp.cumsum(ac, axis=2)
        rel = jnp.where(tri, b[:, :, :, None, :] - b[:, :, None, :, :], -jnp.inf)
        intra = jnp.einsum('bhid,bhjd,bhijd->bhij', qc, kc, jnp.exp(rel))
        o = jnp.einsum('bhij,bhjv->bhiv', intra, vc) + jnp.einsum('bhid,bhdv->bhiv', qc * jnp.exp(b), s)
        b_end = b[:, :, -1:, :]
        s = jnp.exp(b_end[:, :, 0, :, None]) * s + jnp.einsum('bhjd,bhjv->bhdv', kc * jnp.exp(b_end - b), vc)
        return s, o

    f32 = lambda t: to_chunks(t.astype(jnp.float32), C)
    s, o = lax.scan(step, s0, (f32(q), f32(k), f32(v), f32(log_a)))
    return from_chunks(o).astype(v.dtype), s


def _flip(t):
    return jnp.flip(t, axis=2)


def even_mixer(u_lat, u_ctx, w_in, w_out, lam_p, diff_g, ret_logit, ret_g, lam_init, tabs_a, tabs_b, with_ctx_out):
    def project(u):
        b, l, _ = u.shape
        qa, ka, va, qr, kr, vr, gr = split_cols(u @ w_in, EVEN_WIDTHS)
        pair = lambda t: t.reshape(b, l, A_HEADS, 2, A_QK).transpose(0, 2, 3, 1, 4)
        return (pair(qa), pair(ka), heads(va, A_HEADS), heads(qr, B_HEADS),
                heads(kr, B_HEADS) * (B_QK ** -0.5), heads(vr, B_HEADS), gr)

    qa_l, ka_l, va_l, qr_l, kr_l, vr_l, gr_l = project(u_lat)
    qa_c, ka_c, va_c, qr_c, kr_c, vr_c, gr_c = project(u_ctx)
    bsz, n_lat = u_lat.shape[0], u_lat.shape[1]

    lp = lam_p.astype(jnp.float32)
    lam = jnp.exp(jnp.sum(lp[0] * lp[1])) - jnp.exp(jnp.sum(lp[2] * lp[3])) + lam_init
    qa_l = rope_2d(qa_l, tabs_a)
    ka_l = rope_2d(ka_l, tabs_a)
    k_all = jnp.concatenate([ka_c, ka_l], axis=3)
    v_all = jnp.concatenate([va_c, va_l], axis=2)
    nb = n_lat // Q_BLOCK
    q_blocks = jnp.moveaxis(qa_l.reshape(bsz, A_HEADS, 2, nb, Q_BLOCK, A_QK), 3, 0)
    o_blocks = lax.map(lambda qb: diff_attend(qb, k_all, v_all, lam), q_blocks)
    oa_l = jnp.moveaxis(o_blocks, 0, 2).reshape(bsz, A_HEADS, n_lat, A_V)

    def diff_post(o):
        return merge(rmsnorm(o, diff_g) * (1.0 - lam_init))

    log_g = jax.nn.log_sigmoid(ret_logit.astype(jnp.float32))
    qr_l = rope_2d(qr_l, tabs_b)
    kr_l = rope_2d(kr_l, tabs_b)
    zero = jnp.zeros((bsz, B_HEADS, B_QK, B_V), jnp.float32)
    oc_f, sc_f = retention_scan(qr_c, kr_c, vr_c, log_g[0], zero)
    oc_b, sc_b = retention_scan(_flip(qr_c), _flip(kr_c), _flip(vr_c), log_g[1], zero)
    ol_f, _ = retention_scan(qr_l, kr_l, vr_l, log_g[0], sc_f)
    ol_b, _ = retention_scan(_flip(qr_l), _flip(kr_l), _flip(vr_l), log_g[1], sc_b)

    def ret_post(o, gate):
        return merge(rmsnorm(o, ret_g[:, None, :])) * jax.nn.silu(gate)

    y_lat = jnp.concatenate([diff_post(oa_l), ret_post(ol_f + _flip(ol_b), gr_l)], axis=-1) @ w_out
    if not with_ctx_out:
        return y_lat, None
    oa_c = diff_attend(qa_c, ka_c, va_c, lam)
    y_ctx = jnp.concatenate([diff_post(oa_c), ret_post(oc_f + _flip(oc_b), gr_c)], axis=-1) @ w_out
    return y_lat, y_ctx


def odd_mixer(u_lat, u_ctx, w_in, w_out, gla_w_up, gla_b, gla_g, with_ctx_out):
    def project(u):
        b, l, _ = u.shape
        q, k, v, r, z = split_cols(u @ w_in, ODD_WIDTHS)
        z = z.reshape(b, l, 2, GLA_RANK)
        logits = jnp.einsum('blnr,nrk->nblk', z, gla_w_up) + gla_b[:, None, None, :]
        log_a = jax.nn.log_sigmoid(logits.astype(jnp.float32)) / GLA_TAU
        log_a = log_a.reshape(2, b, l, C_HEADS, C_QK).transpose(0, 1, 3, 2, 4)
        return heads(q, C_HEADS) * (C_QK ** -0.5), heads(k, C_HEADS), heads(v, C_HEADS), r, log_a

    q_l, k_l, v_l, r_l, a_l = project(u_lat)
    q_c, k_c, v_c, r_c, a_c = project(u_ctx)
    bsz = u_lat.shape[0]
    zero = jnp.zeros((bsz, C_HEADS, C_QK, C_V), jnp.float32)
    oc_f, sc_f = gla_scan(q_c, k_c, v_c, a_c[0], zero)
    oc_b, sc_b = gla_scan(_flip(q_c), _flip(k_c), _flip(v_c), _flip(a_c[1]), zero)
    ol_f, _ = gla_scan(q_l, k_l, v_l, a_l[0], sc_f)
    ol_b, _ = gla_scan(_flip(q_l), _flip(k_l), _flip(v_l), _flip(a_l[1]), sc_b)

    def post(o, gate):
        return (merge(rmsnorm(o, gla_g[:, None, :])) * jax.nn.silu(gate)) @ w_out

    y_lat = post(ol_f + _flip(ol_b), r_l)
    if not with_ctx_out:
        return y_lat, None
    return y_lat, post(oc_f + _flip(oc_b), r_c)


def setup_inputs(seed: int = 0) -> dict:
    key = jax.random.key(seed)
    ks = jax.random.split(key, 24)
    f32 = jnp.float32
    nrm = lambda k, shape, s: jax.random.normal(k, shape, f32) * s
    n_even = (DEPTH + 1) // 2
    n_odd = DEPTH // 2
    ret_init = jnp.log(2.0 ** (5.0 + jnp.arange(B_HEADS, dtype=f32)) - 1.0)
    return {
        "x": nrm(ks[0], (BATCH, SEQ, D_MODEL), 1.0),
        "c": nrm(ks[1], (BATCH, D_MODEL), 1.0),
        "ctx": nrm(ks[2], (BATCH, CTX_LEN, D_MODEL), 1.0),
        "c_ctx": nrm(ks[3], (D_MODEL,), 1.0),
        "norm_g": 1.0 + nrm(ks[4], (DEPTH, 6, D_MODEL), 0.02),
        "ada_down": nrm(ks[5], (DEPTH, D_MODEL, ADA_RANK), D_MODEL ** -0.5),
        "ada_up": nrm(ks[6], (DEPTH, ADA_RANK, N_MOD * D_MODEL), 0.5 * ADA_RANK ** -0.5),
        "ada_b": nrm(ks[7], (DEPTH, N_MOD * D_MODEL), 0.02),
        "ffn_in": nrm(ks[8], (DEPTH, 2, D_MODEL, 2 * D_FF), D_MODEL ** -0.5),
        "ffn_out": nrm(ks[9], (DEPTH, 2, D_FF, D_MODEL), D_FF ** -0.5),
        "even_w_in": nrm(ks[10], (n_even, D_MODEL, EVEN_IN), D_MODEL ** -0.5),
        "even_w_out": nrm(ks[11], (n_even, MIX_WIDTH, D_MODEL), MIX_WIDTH ** -0.5),
        "diff_lambda": nrm(ks[12], (n_even, 4, A_QK), 0.1),
        "diff_g": 1.0 + nrm(ks[13], (n_even, A_V), 0.02),
        "ret_logit": ret_init[None, None, :] + nrm(ks[14], (n_even, 2, B_HEADS), 0.05),
        "ret_g": 1.0 + nrm(ks[15], (n_even, B_HEADS, B_V), 0.02),
        "odd_w_in": nrm(ks[16], (n_odd, D_MODEL, ODD_IN), D_MODEL ** -0.5),
        "gla_w_up": nrm(ks[17], (n_odd, 2, GLA_RANK, C_QK_W), GLA_RANK ** -0.5),
        "gla_b": 2.0 + nrm(ks[18], (n_odd, 2, C_QK_W), 0.1),
        "gla_g": 1.0 + nrm(ks[19], (n_odd, C_HEADS, C_V), 0.02),
        "odd_w_out": nrm(ks[20], (n_odd, C_WIDTH, D_MODEL), C_WIDTH ** -0.5),
    }


def reference(x, c, ctx, c_ctx, norm_g, ada_down, ada_up, ada_b, ffn_in, ffn_out,
              even_w_in, even_w_out, diff_lambda, diff_g, ret_logit, ret_g,
              odd_w_in, gla_w_up, gla_b, gla_g, odd_w_out):
    n_lat = x.shape[1]
    rows = n_lat // GRID_W
    tabs_a = axial_rope_tables(rows, A_QK)
    tabs_b = axial_rope_tables(rows, B_QK)
    h_lat, h_ctx = x, ctx
    for l in range(DEPTH):
        with_ctx_out = l < DEPTH - 1
        g = norm_g[l]
        m_lat = adaln(c, ada_down[l], ada_up[l], ada_b[l])
        m_ctx = adaln(c_ctx[None, :], ada_down[l], ada_up[l], ada_b[l])

        h_lat = h_lat + 0.5 * ffn_sublayer(h_lat, m_lat, 0, g[0], g[1], ffn_in[l, 0], ffn_out[l, 0])
        h_ctx = h_ctx + 0.5 * ffn_sublayer(h_ctx, m_ctx, 0, g[0], g[1], ffn_in[l, 0], ffn_out[l, 0])

        u_lat = modulate(h_lat, g[2], m_lat[:, 3], m_lat[:, 4])
        u_ctx = modulate(h_ctx, g[2], m_ctx[:, 3], m_ctx[:, 4])
        if l % 2 == 0:
            e = l // 2
            lam_init = 0.8 - 0.6 * math.exp(-0.3 * l)
            y_lat, y_ctx = even_mixer(u_lat, u_ctx, even_w_in[e], even_w_out[e], diff_lambda[e], diff_g[e],
                                      ret_logit[e], ret_g[e], lam_init, tabs_a, tabs_b, with_ctx_out)
        else:
            o = l // 2
            y_lat, y_ctx = odd_mixer(u_lat, u_ctx, odd_w_in[o], odd_w_out[o], gla_w_up[o], gla_b[o],
                                     gla_g[o], with_ctx_out)
        h_lat = h_lat + m_lat[:, 5][:, None, :] * rmsnorm(y_lat, g[3])

        h_lat = h_lat + 0.5 * ffn_sublayer(h_lat, m_lat, 6, g[4], g[5], ffn_in[l, 1], ffn_out[l, 1])
        if with_ctx_out:
            h_ctx = h_ctx + m_ctx[:, 5][:, None, :] * rmsnorm(y_ctx, g[3])
            h_ctx = h_ctx + 0.5 * ffn_sublayer(h_ctx, m_ctx, 6, g[4], g[5], ffn_in[l, 1], ffn_out[l, 1])
    return h_lat
```

```python
import functools
import math

import jax
import jax.numpy as jnp
from jax import lax
from jax.experimental import pallas as pl
from jax.experimental.pallas import tpu as pltpu

F32 = jnp.float32
BF16 = jnp.bfloat16
EPS = 1e-6
GRID_W = 64
ROPE_BASE = 10000.0
GLA_TAU = 16.0
LANES = 128
MOD_ROWS = 8
N_MOD = 9
_HI = lax.Precision.HIGHEST
_NT = (((1,), (1,)), ((), ()))
_TN = (((0,), (0,)), ((), ()))


def _cp(n_axes, vmem_mb):
    return pltpu.CompilerParams(dimension_semantics=("arbitrary",) * n_axes,
                                vmem_limit_bytes=vmem_mb * 2 ** 20)


def _pick(n, cands):
    for c in cands:
        if n % c == 0:
            return c
    raise ValueError(f"no tile in {cands} divides {n}")


def _rms(x, g):
    return x * lax.rsqrt(jnp.mean(x * x, axis=-1, keepdims=True) + EPS) * g


def _silu(x):
    return x * jax.nn.sigmoid(x)


def _dot(a, b):
    return jnp.dot(a, b, preferred_element_type=F32)


def _dot_nt(a, b):
    return lax.dot_general(a, b, _NT, preferred_element_type=F32)


def _adaln_kernel(c_ref, down_ref, up_ref, b_ref, o_ref, t_ref):
    @pl.when(pl.program_id(1) == 0)
    def _():
        t_ref[...] = jnp.dot(_silu(c_ref[...]), down_ref[0], precision=_HI,
                             preferred_element_type=F32)

    o_ref[0] = jnp.dot(t_ref[...], up_ref[0], precision=_HI,
                       preferred_element_type=F32) + b_ref[0]


def _adaln_all(c8, ada_down, ada_up, ada_b):
    depth, d, r = ada_down.shape
    n = ada_up.shape[-1]
    tn = d
    out = pl.pallas_call(
        _adaln_kernel,
        grid=(depth, n // tn),
        in_specs=[
            pl.BlockSpec((MOD_ROWS, d), lambda l, j: (0, 0)),
            pl.BlockSpec((1, d, r), lambda l, j: (l, 0, 0)),
            pl.BlockSpec((1, r, tn), lambda l, j: (l, 0, j)),
            pl.BlockSpec((1, 1, tn), lambda l, j: (l, 0, j)),
        ],
        out_specs=pl.BlockSpec((1, MOD_ROWS, tn), lambda l, j: (l, 0, j)),
        out_shape=jax.ShapeDtypeStruct((depth, MOD_ROWS, n), F32),
        scratch_shapes=[pltpu.VMEM((MOD_ROWS, r), F32)],
        compiler_params=_cp(2, 40),
        name="adaln",
    )(c8, ada_down, ada_up, ada_b.reshape(depth, 1, n))
    return out.reshape(depth, MOD_ROWS, N_MOD, d)


def _modulated(h, g_ref, gi, m, j):
    return _rms(h, g_ref[gi:gi + 1, :]) * (1.0 + m[j + 1:j + 2, :]) + m[j:j + 1, :]


def _pre_kernel(h_ref, g_ref, m_ref, u_ref, *, gi, j):
    u_ref[...] = _modulated(h_ref[...], g_ref, gi, m_ref[0], j).astype(BF16)


def _post_pre_kernel(y_ref, h_ref, g_ref, m_ref, g2_ref, m2_ref, hn_ref, u_ref,
                     *, gi, jg, coef, gi2, j2):
    m = m_ref[0]
    hn = h_ref[...] + coef * (m[jg:jg + 1, :] * _rms(y_ref[...], g_ref[gi:gi + 1, :]))
    hn_ref[...] = hn
    u_ref[...] = _modulated(hn, g2_ref, gi2, m2_ref[0], j2).astype(BF16)


def _post_kernel(y_ref, h_ref, g_ref, m_ref, hn_ref, *, gi, jg, coef):
    m = m_ref[0]
    hn_ref[...] = h_ref[...] + coef * (m[jg:jg + 1, :] * _rms(y_ref[...], g_ref[gi:gi + 1, :]))


class _Rows:
    def __init__(self, bsz, seq, ctx_len, d):
        self.tr = _pick(math.gcd(seq, ctx_len), (256, 128))
        self.d = d
        per_batch = seq // self.tr
        self.row = pl.BlockSpec((self.tr, d), lambda i: (i, 0))
        self.gain = pl.BlockSpec((6, d), lambda i: (0, 0))
        self.mod = pl.BlockSpec((1, N_MOD, d), lambda i: (jnp.minimum(i // per_batch, bsz), 0, 0))

    def pre(self, h, g, m, gi, j):
        rows = h.shape[0]
        return pl.pallas_call(
            functools.partial(_pre_kernel, gi=gi, j=j),
            grid=(rows // self.tr,),
            in_specs=[self.row, self.gain, self.mod],
            out_specs=self.row,
            out_shape=jax.ShapeDtypeStruct((rows, self.d), BF16),
            compiler_params=_cp(1, 32),
            name="prenorm",
        )(h, g, m)

    def post_pre(self, y, h, g, m, gi, jg, coef, g2, m2, gi2, j2):
        rows = y.shape[0]
        return pl.pallas_call(
            functools.partial(_post_pre_kernel, gi=gi, jg=jg, coef=coef, gi2=gi2, j2=j2),
            grid=(rows // self.tr,),
            in_specs=[self.row, self.row, self.gain, self.mod, self.gain, self.mod],
            out_specs=[self.row, self.row],
            out_shape=[jax.ShapeDtypeStruct((rows, self.d), F32),
                       jax.ShapeDtypeStruct((rows, self.d), BF16)],
            compiler_params=_cp(1, 48),
            name="post_pre",
        )(y, h, g, m, g2, m2)

    def post(self, y, h, g, m, gi, jg, coef):
        rows = y.shape[0]
        return pl.pallas_call(
            functools.partial(_post_kernel, gi=gi, jg=jg, coef=coef),
            grid=(rows // self.tr,),
            in_specs=[self.row, self.row, self.gain, self.mod],
            out_specs=self.row,
            out_shape=jax.ShapeDtypeStruct((rows, self.d), F32),
            compiler_params=_cp(1, 48),
            name="post",
        )(y, h, g, m)


def _mm_kernel(x_ref, w_ref, o_ref):
    o_ref[...] = _dot(x_ref[...], w_ref[...]).astype(o_ref.dtype)


def _matmul(x, w, rows, out_dtype, bm, bn):
    k = x.shape[1]
    n = w.shape[1]
    return pl.pallas_call(
        _mm_kernel,
        grid=(rows // bm, n // bn),
        in_specs=[pl.BlockSpec((bm, k), lambda i, j: (i, 0)),
                  pl.BlockSpec((k, bn), lambda i, j: (0, j))],
        out_specs=pl.BlockSpec((bm, bn), lambda i, j: (i, j)),
        out_shape=jax.ShapeDtypeStruct((rows, n), out_dtype),
        compiler_params=_cp(2, 56),
        name="matmul",
    )(x, w)


def _ffn_in_kernel(x_ref, wg_ref, wu_ref, z_ref):
    x = x_ref[...]
    gate = _dot(x, wg_ref[...])
    up = _dot(x, wu_ref[...])
    z_ref[...] = (_silu(gate) * up).astype(BF16)


def _ffn_in(x, w_in, rows, bm):
    k = x.shape[1]
    f = w_in.shape[1] // 2
    bn = _pick(f, (512, 256, 128))
    nj = f // bn
    return pl.pallas_call(
        _ffn_in_kernel,
        grid=(rows // bm, nj),
        in_specs=[pl.BlockSpec((bm, k), lambda i, j: (i, 0)),
                  pl.BlockSpec((k, bn), lambda i, j: (0, j)),
                  pl.BlockSpec((k, bn), lambda i, j: (0, j + nj))],
        out_specs=pl.BlockSpec((bm, bn), lambda i, j: (i, j)),
        out_shape=jax.ShapeDtypeStruct((rows, f), BF16),
        compiler_params=_cp(2, 56),
        name="ffn_in",
    )(x, w_in, w_in)


def _rope_tables(seq, head_dim):
    rows = seq // GRID_W
    row = jnp.repeat(jnp.arange(rows), GRID_W).astype(F32)
    col = jnp.tile(jnp.arange(GRID_W), rows).astype(F32)
    axis_dim = head_dim // 2
    inv_freq = ROPE_BASE ** (-jnp.arange(0, axis_dim, 2, dtype=F32) / axis_dim)
    ang_r = row[:, None] * inv_freq[None, :]
    ang_c = col[:, None] * inv_freq[None, :]
    zero = jnp.zeros_like(ang_r)
    cos = jnp.concatenate([jnp.cos(ang_r), jnp.cos(ang_r), jnp.cos(ang_c), jnp.cos(ang_c)], axis=-1)
    sin_hi = jnp.concatenate([-jnp.sin(ang_r), zero, -jnp.sin(ang_c), zero], axis=-1)
    sin_lo = jnp.concatenate([zero, jnp.sin(ang_r), zero, jnp.sin(ang_c)], axis=-1)
    tab = jnp.stack([cos, sin_hi, sin_lo])
    return jnp.tile(tab, (1, 1, LANES // head_dim))


def _rope_kernel(p_ref, t_ref, o_ref, *, n_lat_tiles, n_a, qa_blocks, scale_a, scale_b, nf_a, nf_b):
    i = pl.program_id(0)
    jo = pl.program_id(1)
    groups = p_ref.shape[1] // LANES
    scale = jnp.where(jo < qa_blocks, scale_a, jnp.where(jo == n_a + 1, scale_b, 1.0)).astype(F32)

    def plain():
        o_ref[...] = (p_ref[...] * scale).astype(BF16)

    def rotated(nf):
        cos, sin_hi, sin_lo = t_ref[0, 0], t_ref[0, 1], t_ref[0, 2]
        for g in range(groups):
            x = p_ref[:, g * LANES:(g + 1) * LANES]
            y = (x * cos + pltpu.roll(x, LANES - nf, 1) * sin_hi + pltpu.roll(x, nf, 1) * sin_lo)
            o_ref[:, g * LANES:(g + 1) * LANES] = (y * scale).astype(BF16)

    is_lat = i < n_lat_tiles
    pl.when(jnp.logical_not(is_lat))(plain)
    pl.when(jnp.logical_and(is_lat, jo < n_a))(functools.partial(rotated, nf_a))
    pl.when(jnp.logical_and(is_lat, jo >= n_a))(functools.partial(rotated, nf_b))


def _attn_kernel(*refs, has_lat, nkv, lam_init, hd):
    if has_lat:
        q_ref, kc_ref, vc_ref, kl_ref, vl_ref, lam_ref, g_ref, o_ref, m_ref, l_ref, acc_ref = refs
    else:
        q_ref, kc_ref, vc_ref, lam_ref, g_ref, _, o_ref, m_ref, l_ref, acc_ref = refs
    kv = pl.program_id(3)

    @pl.when(kv == 0)
    def _():
        m_ref[...] = jnp.full(m_ref.shape, -jnp.inf, F32)
        l_ref[...] = jnp.zeros(l_ref.shape, F32)
        acc_ref[...] = jnp.zeros(acc_ref.shape, F32)

    q = q_ref[...]
    lane = lax.broadcasted_iota(jnp.int32, q.shape, 1)
    zero = jnp.zeros_like(q)
    q_maps = (jnp.where(lane < hd, q, zero), jnp.where(lane >= hd, q, zero))

    def process(k, v):
        for mi in range(2):
            s = _dot_nt(q_maps[mi], k)
            m_prev = m_ref[mi]
            m_new = jnp.maximum(m_prev, jnp.max(s, axis=-1, keepdims=True))
            alpha = jnp.exp(m_prev - m_new)
            p = jnp.exp(s - m_new)
            l_ref[mi] = alpha * l_ref[mi] + jnp.sum(p, axis=-1, keepdims=True)
            acc_ref[mi] = alpha * acc_ref[mi] + _dot(p.astype(BF16), v)
            m_ref[mi] = m_new

    @pl.when(kv == 0)
    def _():
        process(kc_ref[...], vc_ref[...].astype(BF16))

    if has_lat:
        process(kl_ref[...], vl_ref[...].astype(BF16))

    @pl.when(kv == nkv - 1)
    def _():
        lp = lam_ref[...]
        lam = (jnp.exp(jnp.sum(lp[0:1] * lp[1:2], axis=-1, keepdims=True))
               - jnp.exp(jnp.sum(lp[2:3] * lp[3:4], axis=-1, keepdims=True)) + lam_init)
        o = acc_ref[0] / l_ref[0] - lam * (acc_ref[1] / l_ref[1])
        o_ref[...] = (_rms(o, g_ref[...]) * (1.0 - lam_init)).astype(BF16)


def _ret_kernel(*refs, reverse, post, chunk):
    if post:
        q_ref, k_ref, v_ref, lg_ref, of_ref, gate_ref, g_ref, mix_ref, o_ref, s_ref = refs
    else:
        q_ref, k_ref, v_ref, lg_ref, o_ref, s_ref = refs

    @pl.when(pl.program_id(2) == 0)
    def _():
        s_ref[...] = jnp.zeros(s_ref.shape, F32)

    c = chunk
    logit = lg_ref[0]
    lg = jnp.minimum(logit, 0.0) - jnp.log1p(jnp.exp(-jnp.abs(logit)))
    row = lax.broadcasted_iota(jnp.int32, (c, 1), 0)
    col = lax.broadcasted_iota(jnp.int32, (1, c), 1)
    tq = (c - 1 - row) if reverse else row
    tk = (c - 1 - col) if reverse else col
    diff = (tq - tk).astype(F32)
    decay = jnp.where(diff >= 0, jnp.exp(lg * jnp.maximum(diff, 0.0)), 0.0)
    tqf = tq.astype(F32)
    q_dec = jnp.exp(lg * (tqf + 1.0))
    k_dec = jnp.exp(lg * (c - 1.0 - tqf))
    chunk_dec = jnp.exp(lg * c)

    q = q_ref[...]
    k = k_ref[...]
    v = v_ref[...].astype(BF16)
    s = s_ref[...]
    inner = _dot_nt(q, k) * decay
    o = _dot(inner.astype(BF16), v) + _dot((q.astype(F32) * q_dec).astype(BF16), s.astype(BF16))
    kd = (k.astype(F32) * k_dec).astype(BF16)
    s_ref[...] = s * chunk_dec + lax.dot_general(kd, v, _TN, preferred_element_type=F32)
    if post:
        gate = gate_ref[...]
        o_ref[...] = (_rms(of_ref[...] + o, g_ref[0]) * _silu(gate)).astype(BF16)
    else:
        o_ref[...] = o


def _gla_kernel(*refs, reverse, post, chunk, dk):
    if post:
        q_ref, k_ref, v_ref, z_ref, wup_ref, bias_ref, of_ref, r_ref, g_ref, o_ref, st_ref = refs
    else:
        q_ref, k_ref, v_ref, z_ref, wup_ref, bias_ref, o_ref, st_ref = refs

    @pl.when(pl.program_id(2) == 0)
    def _():
        st_ref[...] = jnp.zeros(st_ref.shape, F32)

    c = chunk
    q = q_ref[...] * (dk ** -0.5)
    k = k_ref[...]
    vb = v_ref[...].astype(BF16)
    logits = jnp.dot(z_ref[...], wup_ref[...], precision=_HI, preferred_element_type=F32) + bias_ref[...]
    a = (jnp.minimum(logits, 0.0) - jnp.log1p(jnp.exp(-jnp.abs(logits)))) * (1.0 / GLA_TAU)

    row = lax.broadcasted_iota(jnp.int32, (c, 1), 0)
    col = lax.broadcasted_iota(jnp.int32, (1, c), 1)
    tq = (c - 1 - row) if reverse else row
    tk = (c - 1 - col) if reverse else col

    def earlier(x, d):
        return pltpu.roll(x, (c - d) if reverse else d, 0)

    def later(x, d):
        return pltpu.roll(x, d if reverse else (c - d), 0)

    b = a
    d = 1
    while d < c:
        b = b + jnp.where(tq >= d, earlier(b, d), 0.0)
        d *= 2

    intra = jnp.where(tq == tk, _dot_nt(q.astype(BF16), k.astype(BF16)), 0.0)
    h_end = b
    s, lvl = 1, 0
    while s < c:
        upper_q = (lax.shift_right_logical(tq, lvl) & 1) == 1
        upper_k = (lax.shift_right_logical(tk, lvl) & 1) == 1
        g_prev = earlier(h_end, s)
        eq = jnp.where(upper_q, jnp.exp(jnp.minimum(b - g_prev, 0.0)), 0.0)
        ek = jnp.where(upper_q, 0.0, jnp.exp(jnp.minimum(h_end - b, 0.0)))
        sc = _dot_nt((q * eq).astype(BF16), (k * ek).astype(BF16))
        same = lax.shift_right_logical(tq, lvl + 1) == lax.shift_right_logical(tk, lvl + 1)
        keep = same & upper_q & jnp.logical_not(upper_k)
        intra = intra + jnp.where(keep, sc, 0.0)
        h_end = jnp.where(upper_q, h_end, later(h_end, s))
        s *= 2
        lvl += 1

    st = st_ref[...]
    o = _dot(intra.astype(BF16), vb) + _dot_nt((q * jnp.exp(b)).astype(BF16), st.astype(BF16))
    k_hat = (k * jnp.exp(h_end - b)).astype(BF16)
    st_ref[...] = st * jnp.exp(h_end[0:1, :]) + lax.dot_general(vb, k_hat, _TN, preferred_element_type=F32)
    if post:
        r = r_ref[...]
        o_ref[...] = (_rms(of_ref[...] + o, g_ref[0]) * _silu(r)).astype(BF16)
    else:
        o_ref[...] = o


def _scan_rows(bsz, seq, ctx_len, chunk, reverse):
    ncc, ncl = ctx_len // chunk, seq // chunk
    ctx_base = (bsz * seq) // chunk

    def rows(b, s):
        if reverse:
            return jnp.where(s < ncc, ctx_base + b * ncc + (ncc - 1 - s), b * ncl + (ncl - 1 - (s - ncc)))
        return jnp.where(s < ncc, ctx_base + b * ncc + s, b * ncl + (s - ncc))

    return rows, ncc + ncl


def kernel(x, c, ctx, c_ctx, norm_g, ada_down, ada_up, ada_b, ffn_in, ffn_out, even_w_in, even_w_out,
           diff_lambda, diff_g, ret_logit, ret_g, odd_w_in, gla_w_up, gla_b, gla_g, odd_w_out):
    bsz, seq, d = x.shape
    ctx_len = ctx.shape[1]
    depth = norm_g.shape[0]
    t_lat, t_all = bsz * seq, bsz * (seq + ctx_len)
    assert bsz + 1 <= MOD_ROWS

    rows = _Rows(bsz, seq, ctx_len, d)
    bm = _pick(math.gcd(t_lat, t_all), (1024, 512, 256, 128))

    c8 = jnp.concatenate([c, c_ctx[None, :], jnp.zeros((MOD_ROWS - bsz - 1, d), F32)], axis=0)
    mods = _adaln_all(c8, ada_down, ada_up, ada_b)

    a_heads, a_qk, a_v = d // 256, 64, 128
    b_heads, b_qk, b_v = d // 512, 128, 256
    cb = d // 4
    tabs = jnp.stack([_rope_tables(seq, a_qk), _rope_tables(seq, b_qk)])
    tq = _pick(seq, (1024, 512, 256, 128))
    ret_chunk = min(256, ctx_len)
    c_heads, c_qk, c_v = d // 512, 256, 512
    gla_rank = gla_w_up.shape[2]
    gla_chunk = min(128, ctx_len)

    def even_mixer(u, e, lam_init):
        w_in = even_w_in[e].astype(BF16)
        p = _matmul(u, w_in, t_all, F32, bm, _pick(w_in.shape[1], (1024, 512, 256, 128)))
        tr = rows.tr
        n_lat_tiles = t_lat // tr
        per_batch = seq // tr
        qk = pl.pallas_call(
            functools.partial(_rope_kernel, n_lat_tiles=n_lat_tiles, n_a=4, qa_blocks=2,
                              scale_a=a_qk ** -0.5, scale_b=b_qk ** -0.5, nf_a=a_qk // 4, nf_b=b_qk // 4),
            grid=(t_all // tr, 6),
            in_specs=[pl.BlockSpec((tr, cb), lambda i, jo: (i, jnp.where(jo < 4, jo, jo + 2))),
                      pl.BlockSpec((1, 3, tr, LANES), lambda i, jo: (jnp.where(jo < 4, 0, 1), 0, i % per_batch, 0))],
            out_specs=pl.BlockSpec((tr, cb), lambda i, jo: (i, jo)),
            out_shape=jax.ShapeDtypeStruct((t_all, 6 * cb), BF16),
            compiler_params=_cp(2, 32),
            name="rope",
        )(p, tabs)

        nq, nkv = seq // tq, seq // tq
        ctx_blk = t_lat // ctx_len
        hb = LANES
        ka0, va0 = 2 * cb // hb, 4 * cb // hb
        lam_p = diff_lambda[e]
        dg = diff_g[e].reshape(1, a_v)
        attn_scratch = [pltpu.VMEM((2, tq, 1), F32), pltpu.VMEM((2, tq, 1), F32), pltpu.VMEM((2, tq, a_v), F32)]
        small = [pl.BlockSpec((4, a_qk), lambda b, h, i, s: (0, 0)), pl.BlockSpec((1, a_v), lambda b, h, i, s: (0, 0))]
        mixed = pl.pallas_call(
            functools.partial(_attn_kernel, has_lat=True, nkv=nkv, lam_init=lam_init, hd=a_qk),
            grid=(bsz, a_heads, nq, nkv),
            in_specs=[pl.BlockSpec((tq, hb), lambda b, h, i, s: (b * nq + i, h)),
                      pl.BlockSpec((ctx_len, hb), lambda b, h, i, s: (ctx_blk + b, ka0 + h)),
                      pl.BlockSpec((ctx_len, hb), lambda b, h, i, s: (ctx_blk + b, va0 + h)),
                      pl.BlockSpec((tq, hb), lambda b, h, i, s: (b * nkv + s, ka0 + h)),
                      pl.BlockSpec((tq, hb), lambda b, h, i, s: (b * nkv + s, va0 + h))] + small,
            out_specs=pl.BlockSpec((tq, hb), lambda b, h, i, s: (b * nq + i, h)),
            out_shape=jax.ShapeDtypeStruct((t_all, d), BF16),
            scratch_shapes=attn_scratch,
            compiler_params=_cp(4, 48),
            name="diff_attn_lat",
        )(qk, qk, p, qk, p, lam_p, dg)
        ctx_scratch = [pltpu.VMEM((2, ctx_len, 1), F32), pltpu.VMEM((2, ctx_len, 1), F32),
                       pltpu.VMEM((2, ctx_len, a_v), F32)]
        mixed = pl.pallas_call(
            functools.partial(_attn_kernel, has_lat=False, nkv=1, lam_init=lam_init, hd=a_qk),
            grid=(bsz, a_heads, 1, 1),
            in_specs=[pl.BlockSpec((ctx_len, hb), lambda b, h, i, s: (ctx_blk + b, h)),
                      pl.BlockSpec((ctx_len, hb), lambda b, h, i, s: (ctx_blk + b, ka0 + h)),
                      pl.BlockSpec((ctx_len, hb), lambda b, h, i, s: (ctx_blk + b, va0 + h))] + small
                     + [pl.BlockSpec(memory_space=pl.ANY)],
            out_specs=pl.BlockSpec((ctx_len, hb), lambda b, h, i, s: (ctx_blk + b, h)),
            out_shape=jax.ShapeDtypeStruct((t_all, d), BF16),
            scratch_shapes=ctx_scratch,
            input_output_aliases={5: 0},
            compiler_params=_cp(4, 32),
            name="diff_attn_ctx",
        )(qk, qk, p, lam_p, dg, mixed)

        rc = ret_chunk
        qr0, kr0 = 4 * cb // b_qk, 5 * cb // b_qk
        vr0, gr0, mo0 = 8 * cb // b_v, 10 * cb // b_v, 2 * cb // b_v
        lgt = ret_logit[e].reshape(2 * b_heads, 1, 1)
        outs = None
        for reverse in (False, True):
            rmap, nsteps = _scan_rows(bsz, seq, ctx_len, rc, reverse)
            di = 1 if reverse else 0
            in_specs = [pl.BlockSpec((rc, b_qk), lambda b, h, s: (rmap(b, s), qr0 + h)),
                        pl.BlockSpec((rc, b_qk), lambda b, h, s: (rmap(b, s), kr0 + h)),
                        pl.BlockSpec((rc, b_v), lambda b, h, s: (rmap(b, s), vr0 + h)),
                        pl.BlockSpec((1, 1, 1), lambda b, h, s: (di * b_heads + h, 0, 0))]
            args = [qk, qk, p, lgt]
            if reverse:
                in_specs += [pl.BlockSpec((rc, b_v), lambda b, h, s: (rmap(b, s), h)),
                             pl.BlockSpec((rc, b_v), lambda b, h, s: (rmap(b, s), gr0 + h)),
                             pl.BlockSpec((1, 1, b_v), lambda b, h, s: (h, 0, 0)),
                             pl.BlockSpec(memory_space=pl.ANY)]
                args += [outs, p, ret_g[e].reshape(b_heads, 1, b_v), mixed]
                out_spec = pl.BlockSpec((rc, b_v), lambda b, h, s: (rmap(b, s), mo0 + h))
                out_shape = jax.ShapeDtypeStruct((t_all, d), BF16)
                aliases = {7: 0}
            else:
                out_spec = pl.BlockSpec((rc, b_v), lambda b, h, s: (rmap(b, s), h))
                out_shape = jax.ShapeDtypeStruct((t_all, b_heads * b_v), F32)
                aliases = {}
            outs = pl.pallas_call(
                functools.partial(_ret_kernel, reverse=reverse, post=reverse, chunk=rc),
                grid=(bsz, b_heads, nsteps),
                in_specs=in_specs,
                out_specs=out_spec,
                out_shape=out_shape,
                scratch_shapes=[pltpu.VMEM((b_qk, b_v), F32)],
                input_output_aliases=aliases,
                compiler_params=_cp(3, 32),
                name="retention_bwd" if reverse else "retention_fwd",
            )(*args)
        return outs, even_w_out[e].astype(BF16)

    def odd_mixer(u, o):
        w_full = odd_w_in[o]
        main_w = c_heads * (2 * c_qk + 2 * c_v)
        w_in = w_full[:, :main_w].astype(BF16)
        w_z = jnp.pad(w_full[:, main_w:], ((0, 0), (0, LANES - 2 * gla_rank))).astype(BF16)
        p = _matmul(u, w_in, t_all, F32, bm, _pick(main_w, (1024, 512, 256, 128)))
        z = _matmul(u, w_z, t_all, F32, bm, LANES)
        gc = gla_chunk
        k0, v0, r0 = c_heads, 2 * c_heads * c_qk // c_v, (2 * c_heads * c_qk + c_heads * c_v) // c_v
        outs = None
        for reverse in (False, True):
            rmap, nsteps = _scan_rows(bsz, seq, ctx_len, gc, reverse)
            di = 1 if reverse else 0
            wup = jnp.zeros((LANES, c_heads * c_qk), F32).at[di * gla_rank:(di + 1) * gla_rank].set(gla_w_up[o, di])
            bias = gla_b[o, di].reshape(1, c_heads * c_qk)
            in_specs = [pl.BlockSpec((gc, c_qk), lambda b, h, s: (rmap(b, s), h)),
                        pl.BlockSpec((gc, c_qk), lambda b, h, s: (rmap(b, s), k0 + h)),
                        pl.BlockSpec((gc, c_v), lambda b, h, s: (rmap(b, s), v0 + h)),
                        pl.BlockSpec((gc, LANES), lambda b, h, s: (rmap(b, s), 0)),
                        pl.BlockSpec((LANES, c_qk), lambda b, h, s: (0, h)),
                        pl.BlockSpec((1, c_qk), lambda b, h, s: (0, h))]
            args = [p, p, p, z, wup, bias]
            out_spec = pl.BlockSpec((gc, c_v), lambda b, h, s: (rmap(b, s), h))
            if reverse:
                in_specs += [pl.BlockSpec((gc, c_v), lambda b, h, s: (rmap(b, s), h)),
                             pl.BlockSpec((gc, c_v), lambda b, h, s: (rmap(b, s), r0 + h)),
                             pl.BlockSpec((1, 1, c_v), lambda b, h, s: (h, 0, 0))]
                args += [outs, p, gla_g[o].reshape(c_heads, 1, c_v)]
                out_shape = jax.ShapeDtypeStruct((t_all, d), BF16)
            else:
                out_shape = jax.ShapeDtypeStruct((t_all, c_heads * c_v), F32)
            outs = pl.pallas_call(
                functools.partial(_gla_kernel, reverse=reverse, post=reverse, chunk=gc, dk=c_qk),
                grid=(bsz, c_heads, nsteps),
                in_specs=in_specs,
                out_specs=out_spec,
                out_shape=out_shape,
                scratch_shapes=[pltpu.VMEM((c_v, c_qk), F32)],
                compiler_params=_cp(3, 32),
                name="gla_bwd" if reverse else "gla_fwd",
            )(*args)
        return outs, odd_w_out[o].astype(BF16)

    bn_d = _pick(d, (512, 256, 128))
    h = jnp.concatenate([x.reshape(t_lat, d), ctx.reshape(bsz * ctx_len, d)], axis=0)
    u = rows.pre(h, norm_g[0], mods[0], 0, 0)
    for l in range(depth):
        g, m = norm_g[l], mods[l]
        last = l == depth - 1
        z = _ffn_in(u, ffn_in[l, 0].astype(BF16), t_all, bm)
        y = _matmul(z, ffn_out[l, 0].astype(BF16), t_all, F32, bm, bn_d)
        h, u = rows.post_pre(y, h, g, m, 1, 2, 0.5, g, m, 2, 3)
        if l % 2 == 0:
            mixed, w_out = even_mixer(u, l // 2, 0.8 - 0.6 * math.exp(-0.3 * l))
        else:
            mixed, w_out = odd_mixer(u, l // 2)
        n_rows = t_lat if last else t_all
        y = _matmul(mixed, w_out, n_rows, F32, bm, bn_d)
        h, u = rows.post_pre(y, h, g, m, 3, 5, 1.0, g, m, 4, 6)
        z = _ffn_in(u, ffn_in[l, 1].astype(BF16), n_rows, bm)
        y = _matmul(z, ffn_out[l, 1].astype(BF16), n_rows, F32, bm, bn_d)
        if last:
            h = rows.post(y, h, g, m, 5, 8, 0.5)
        else:
            h, u = rows.post_pre(y, h, g, m, 5, 8, 0.5, norm_g[l + 1], mods[l + 1], 0, 0)
    return h.reshape(bsz, seq, d)
```

```python
import functools
import math

import jax
import jax.numpy as jnp
from jax import lax
from jax.experimental import pallas as pl
from jax.experimental.pallas import tpu as pltpu

F32 = jnp.float32
BF16 = jnp.bfloat16
EPS = 1e-6
GRID_W = 64
ROPE_BASE = 10000.0
GLA_TAU = 16.0
LANES = 128
MOD_ROWS = 8
N_MOD = 9
_HI = lax.Precision.HIGHEST
_NT = (((1,), (1,)), ((), ()))
_TN = (((0,), (0,)), ((), ()))


def _cp(n_axes, vmem_mb):
    return pltpu.CompilerParams(dimension_semantics=("arbitrary",) * n_axes,
                                vmem_limit_bytes=vmem_mb * 2 ** 20)


def _pick(n, cands):
    for c in cands:
        if n % c == 0:
            return c
    raise ValueError(f"no tile in {cands} divides {n}")


def _rms(x, g):
    return x * lax.rsqrt(jnp.mean(x * x, axis=-1, keepdims=True) + EPS) * g


def _silu(x):
    return x * jax.nn.sigmoid(x)


def _dot(a, b):
    return jnp.dot(a, b, preferred_element_type=F32)


def _dot_nt(a, b):
    return lax.dot_general(a, b, _NT, preferred_element_type=F32)


def _adaln_kernel(c_ref, down_ref, up_ref, b_ref, o_ref, t_ref):
    @pl.when(pl.program_id(1) == 0)
    def _():
        t_ref[...] = jnp.dot(_silu(c_ref[...]), down_ref[0], precision=_HI,
                             preferred_element_type=F32)

    o_ref[0] = jnp.dot(t_ref[...], up_ref[0], precision=_HI,
                       preferred_element_type=F32) + b_ref[0]


def _adaln_all(c8, ada_down, ada_up, ada_b):
    depth, d, r = ada_down.shape
    n = ada_up.shape[-1]
    tn = d
    out = pl.pallas_call(
        _adaln_kernel,
        grid=(depth, n // tn),
        in_specs=[
            pl.BlockSpec((MOD_ROWS, d), lambda l, j: (0, 0)),
            pl.BlockSpec((1, d, r), lambda l, j: (l, 0, 0)),
            pl.BlockSpec((1, r, tn), lambda l, j: (l, 0, j)),
            pl.BlockSpec((1, 1, tn), lambda l, j: (l, 0, j)),
        ],
        out_specs=pl.BlockSpec((1, MOD_ROWS, tn), lambda l, j: (l, 0, j)),
        out_shape=jax.ShapeDtypeStruct((depth, MOD_ROWS, n), F32),
        scratch_shapes=[pltpu.VMEM((MOD_ROWS, r), F32)],
        compiler_params=_cp(2, 40),
        name="adaln",
    )(c8, ada_down, ada_up, ada_b.reshape(depth, 1, n))
    return out.reshape(depth, MOD_ROWS, N_MOD, d)


def _modulated(h, g_ref, gi, m, j):
    return _rms(h, g_ref[gi:gi + 1, :]) * (1.0 + m[j + 1:j + 2, :]) + m[j:j + 1, :]


def _pre_kernel(h_ref, g_ref, m_ref, u_ref, *, gi, j):
    u_ref[...] = _modulated(h_ref[...], g_ref, gi, m_ref[0], j).astype(BF16)


def _post_pre_kernel(y_ref, h_ref, g_ref, m_ref, g2_ref, m2_ref, hn_ref, u_ref,
                     *, gi, jg, coef, gi2, j2):
    m = m_ref[0]
    hn = h_ref[...] + coef * (m[jg:jg + 1, :] * _rms(y_ref[...].astype(F32), g_ref[gi:gi + 1, :]))
    hn_ref[...] = hn
    u_ref[...] = _modulated(hn, g2_ref, gi2, m2_ref[0], j2).astype(BF16)


def _post_kernel(y_ref, h_ref, g_ref, m_ref, hn_ref, *, gi, jg, coef):
    m = m_ref[0]
    hn_ref[...] = h_ref[...] + coef * (m[jg:jg + 1, :] * _rms(y_ref[...].astype(F32), g_ref[gi:gi + 1, :]))


class _Rows:
    def __init__(self, bsz, seq, ctx_len, d):
        self.tr = _pick(math.gcd(seq, ctx_len), (256, 128))
        self.d = d
        per_batch = seq // self.tr
        self.row = pl.BlockSpec((self.tr, d), lambda i: (i, 0))
        self.gain = pl.BlockSpec((6, d), lambda i: (0, 0))
        self.mod = pl.BlockSpec((1, N_MOD, d), lambda i: (jnp.minimum(i // per_batch, bsz), 0, 0))

    def pre(self, h, g, m, gi, j):
        rows = h.shape[0]
        return pl.pallas_call(
            functools.partial(_pre_kernel, gi=gi, j=j),
            grid=(rows // self.tr,),
            in_specs=[self.row, self.gain, self.mod],
            out_specs=self.row,
            out_shape=jax.ShapeDtypeStruct((rows, self.d), BF16),
            compiler_params=_cp(1, 32),
            name="prenorm",
        )(h, g, m)

    def post_pre(self, y, h, g, m, gi, jg, coef, g2, m2, gi2, j2):
        rows = y.shape[0]
        return pl.pallas_call(
            functools.partial(_post_pre_kernel, gi=gi, jg=jg, coef=coef, gi2=gi2, j2=j2),
            grid=(rows // self.tr,),
            in_specs=[self.row, self.row, self.gain, self.mod, self.gain, self.mod],
            out_specs=[self.row, self.row],
            out_shape=[jax.ShapeDtypeStruct((rows, self.d), F32),
                       jax.ShapeDtypeStruct((rows, self.d), BF16)],
            compiler_params=_cp(1, 48),
            name="post_pre",
        )(y, h, g, m, g2, m2)

    def post(self, y, h, g, m, gi, jg, coef):
        rows = y.shape[0]
        return pl.pallas_call(
            functools.partial(_post_kernel, gi=gi, jg=jg, coef=coef),
            grid=(rows // self.tr,),
            in_specs=[self.row, self.row, self.gain, self.mod],
            out_specs=self.row,
            out_shape=jax.ShapeDtypeStruct((rows, self.d), F32),
            compiler_params=_cp(1, 48),
            name="post",
        )(y, h, g, m)


def _mm_kernel(x_ref, w_ref, o_ref):
    o_ref[...] = _dot(x_ref[...], w_ref[...]).astype(o_ref.dtype)


def _matmul(x, w, rows, out_dtype, bm, bn):
    k = x.shape[1]
    n = w.shape[1]
    return pl.pallas_call(
        _mm_kernel,
        grid=(rows // bm, n // bn),
        in_specs=[pl.BlockSpec((bm, k), lambda i, j: (i, 0)),
                  pl.BlockSpec((k, bn), lambda i, j: (0, j))],
        out_specs=pl.BlockSpec((bm, bn), lambda i, j: (i, j)),
        out_shape=jax.ShapeDtypeStruct((rows, n), out_dtype),
        compiler_params=_cp(2, 56),
        name="matmul",
    )(x, w)


def _ffn_in_kernel(x_ref, wg_ref, wu_ref, z_ref):
    x = x_ref[...]
    gate = _dot(x, wg_ref[...])
    up = _dot(x, wu_ref[...])
    z_ref[...] = (_silu(gate) * up).astype(BF16)


def _ffn_in(x, w_in, rows, bm):
    k = x.shape[1]
    f = w_in.shape[1] // 2
    bn = _pick(f, (512, 256, 128))
    nj = f // bn
    return pl.pallas_call(
        _ffn_in_kernel,
        grid=(rows // bm, nj),
        in_specs=[pl.BlockSpec((bm, k), lambda i, j: (i, 0)),
                  pl.BlockSpec((k, bn), lambda i, j: (0, j)),
                  pl.BlockSpec((k, bn), lambda i, j: (0, j + nj))],
        out_specs=pl.BlockSpec((bm, bn), lambda i, j: (i, j)),
        out_shape=jax.ShapeDtypeStruct((rows, f), BF16),
        compiler_params=_cp(2, 56),
        name="ffn_in",
    )(x, w_in, w_in)


def _rope_tables(seq, head_dim):
    rows = seq // GRID_W
    row = jnp.repeat(jnp.arange(rows), GRID_W).astype(F32)
    col = jnp.tile(jnp.arange(GRID_W), rows).astype(F32)
    axis_dim = head_dim // 2
    inv_freq = ROPE_BASE ** (-jnp.arange(0, axis_dim, 2, dtype=F32) / axis_dim)
    ang_r = row[:, None] * inv_freq[None, :]
    ang_c = col[:, None] * inv_freq[None, :]
    zero = jnp.zeros_like(ang_r)
    cos = jnp.concatenate([jnp.cos(ang_r), jnp.cos(ang_r), jnp.cos(ang_c), jnp.cos(ang_c)], axis=-1)
    sin_hi = jnp.concatenate([-jnp.sin(ang_r), zero, -jnp.sin(ang_c), zero], axis=-1)
    sin_lo = jnp.concatenate([zero, jnp.sin(ang_r), zero, jnp.sin(ang_c)], axis=-1)
    tab = jnp.stack([cos, sin_hi, sin_lo])
    return jnp.tile(tab, (1, 1, LANES // head_dim))


def _rope_kernel(p_ref, t_ref, o_ref, *, n_lat_tiles, rot, scales):
    i = pl.program_id(0)
    jo = pl.program_id(1)
    groups = p_ref.shape[1] // LANES
    scale = jnp.float32(scales[-1])
    for idx in range(len(scales) - 1):
        scale = jnp.where(jo == idx, jnp.float32(scales[idx]), scale)

    def plain():
        o_ref[...] = (p_ref[...] * scale).astype(BF16)

    def rotated(nf):
        cos, sin_hi, sin_lo = t_ref[0, 0], t_ref[0, 1], t_ref[0, 2]
        for g in range(groups):
            x = p_ref[:, g * LANES:(g + 1) * LANES]
            y = (x * cos + pltpu.roll(x, LANES - nf, 1) * sin_hi + pltpu.roll(x, nf, 1) * sin_lo)
            o_ref[:, g * LANES:(g + 1) * LANES] = (y * scale).astype(BF16)

    is_lat = i < n_lat_tiles
    use_plain = jnp.logical_not(is_lat)
    for nf in sorted(set(rot)):
        match = functools.reduce(jnp.logical_or, [jo == idx for idx, r in enumerate(rot) if r == nf])
        if nf == 0:
            use_plain = jnp.logical_or(use_plain, match)
        else:
            pl.when(jnp.logical_and(is_lat, match))(functools.partial(rotated, nf))
    pl.when(use_plain)(plain)


def _attn_kernel(*refs, lam_init, hd, rsub):
    q_ref, k_ref, v_ref, lam_ref, g_ref = refs[:5]
    o_ref = refs[-1]
    g = g_ref[...]
    lp = lam_ref[...]
    lam = (jnp.exp(jnp.sum(lp[0:1] * lp[1:2], axis=-1, keepdims=True))
           - jnp.exp(jnp.sum(lp[2:3] * lp[3:4], axis=-1, keepdims=True)) + lam_init)

    def body(r, carry):
        rows = pl.ds(pl.multiple_of(r * rsub, rsub), rsub)
        q = q_ref[rows, :]
        lane = lax.broadcasted_iota(jnp.int32, q.shape, 1)
        zero = jnp.zeros_like(q)
        probs = []
        for qm, w in ((jnp.where(lane < hd, q, zero), 1.0), (jnp.where(lane >= hd, q, zero), lam)):
            s = _dot_nt(qm, k_ref[...])
            p = jnp.exp2(s - jnp.max(s, axis=-1, keepdims=True))
            probs.append(p * (w / jnp.sum(p, axis=-1, keepdims=True)))
        o = _dot((probs[0] - probs[1]).astype(BF16), v_ref[...])
        o_ref[rows, :] = (_rms(o, g) * (1.0 - lam_init)).astype(BF16)
        return carry

    lax.fori_loop(0, q_ref.shape[0] // rsub, body, 0)


def _ret_kernel(*refs, reverse, post, chunk):
    if post:
        q_ref, k_ref, v_ref, lg_ref, of_ref, gate_ref, g_ref, mix_ref, o_ref, s_ref = refs
    else:
        q_ref, k_ref, v_ref, lg_ref, o_ref, s_ref = refs

    @pl.when(pl.program_id(2) == 0)
    def _():
        s_ref[...] = jnp.zeros(s_ref.shape, F32)

    c = chunk
    logit = lg_ref[0]
    lg = jnp.minimum(logit, 0.0) - jnp.log1p(jnp.exp(-jnp.abs(logit)))
    row = lax.broadcasted_iota(jnp.int32, (c, 1), 0)
    col = lax.broadcasted_iota(jnp.int32, (1, c), 1)
    tq = (c - 1 - row) if reverse else row
    tk = (c - 1 - col) if reverse else col
    diff = (tq - tk).astype(F32)
    decay = jnp.where(diff >= 0, jnp.exp(lg * jnp.maximum(diff, 0.0)), 0.0)
    tqf = tq.astype(F32)
    q_dec = jnp.exp(lg * (tqf + 1.0))
    k_dec = jnp.exp(lg * (c - 1.0 - tqf))
    chunk_dec = jnp.exp(lg * c)

    q = q_ref[...]
    k = k_ref[...]
    v = v_ref[...].astype(BF16)
    s = s_ref[...]
    inner = _dot_nt(q, k) * decay
    o = _dot(inner.astype(BF16), v) + _dot((q.astype(F32) * q_dec).astype(BF16), s.astype(BF16))
    kd = (k.astype(F32) * k_dec).astype(BF16)
    s_ref[...] = s * chunk_dec + lax.dot_general(kd, v, _TN, preferred_element_type=F32)
    if post:
        gate = gate_ref[...]
        o_ref[...] = (_rms(of_ref[...] + o, g_ref[0]) * _silu(gate)).astype(BF16)
    else:
        o_ref[...] = o


def _gla_kernel(*refs, reverse, post, chunk, dk):
    if post:
        q_ref, k_ref, v_ref, z_ref, wup_ref, bias_ref, of_ref, r_ref, g_ref, o_ref, st_ref = refs
    else:
        q_ref, k_ref, v_ref, z_ref, wup_ref, bias_ref, o_ref, st_ref = refs

    @pl.when(pl.program_id(2) == 0)
    def _():
        st_ref[...] = jnp.zeros(st_ref.shape, F32)

    c = chunk
    q = q_ref[...] * (dk ** -0.5)
    k = k_ref[...]
    vb = v_ref[...].astype(BF16)
    logits = jnp.dot(z_ref[...], wup_ref[...], precision=_HI, preferred_element_type=F32) + bias_ref[...]
    a = (jnp.minimum(logits, 0.0) - jnp.log1p(jnp.exp(-jnp.abs(logits)))) * (1.0 / GLA_TAU)

    row = lax.broadcasted_iota(jnp.int32, (c, 1), 0)
    col = lax.broadcasted_iota(jnp.int32, (1, c), 1)
    tq = (c - 1 - row) if reverse else row
    tk = (c - 1 - col) if reverse else col

    def earlier(x, d):
        return pltpu.roll(x, (c - d) if reverse else d, 0)

    def later(x, d):
        return pltpu.roll(x, d if reverse else (c - d), 0)

    b = a
    d = 1
    while d < c:
        b = b + jnp.where(tq >= d, earlier(b, d), 0.0)
        d *= 2

    intra = jnp.where(tq == tk, _dot_nt(q.astype(BF16), k.astype(BF16)), 0.0)
    h_end = b
    s, lvl = 1, 0
    while s < c:
        upper_q = (lax.shift_right_logical(tq, lvl) & 1) == 1
        upper_k = (lax.shift_right_logical(tk, lvl) & 1) == 1
        g_prev = earlier(h_end, s)
        eq = jnp.where(upper_q, jnp.exp(jnp.minimum(b - g_prev, 0.0)), 0.0)
        ek = jnp.where(upper_q, 0.0, jnp.exp(jnp.minimum(h_end - b, 0.0)))
        sc = _dot_nt((q * eq).astype(BF16), (k * ek).astype(BF16))
        same = lax.shift_right_logical(tq, lvl + 1) == lax.shift_right_logical(tk, lvl + 1)
        keep = same & upper_q & jnp.logical_not(upper_k)
        intra = intra + jnp.where(keep, sc, 0.0)
        h_end = jnp.where(upper_q, h_end, later(h_end, s))
        s *= 2
        lvl += 1

    st = st_ref[...]
    o = _dot(intra.astype(BF16), vb) + _dot_nt((q * jnp.exp(b)).astype(BF16), st.astype(BF16))
    k_hat = (k * jnp.exp(h_end - b)).astype(BF16)
    st_ref[...] = st * jnp.exp(h_end[0:1, :]) + lax.dot_general(vb, k_hat, _TN, preferred_element_type=F32)
    if post:
        r = r_ref[...]
        o_ref[...] = (_rms(of_ref[...] + o, g_ref[0]) * _silu(r)).astype(BF16)
    else:
        o_ref[...] = o


def _scan_rows(bsz, seq, ctx_len, chunk, reverse):
    ncc, ncl = ctx_len // chunk, seq // chunk
    ctx_base = (bsz * seq) // chunk

    def rows(b, s):
        if reverse:
            return jnp.where(s < ncc, ctx_base + b * ncc + (ncc - 1 - s), b * ncl + (ncl - 1 - (s - ncc)))
        return jnp.where(s < ncc, ctx_base + b * ncc + s, b * ncl + (s - ncc))

    return rows, ncc + ncl


def kernel(x, c, ctx, c_ctx, norm_g, ada_down, ada_up, ada_b, ffn_in, ffn_out, even_w_in, even_w_out,
           diff_lambda, diff_g, ret_logit, ret_g, odd_w_in, gla_w_up, gla_b, gla_g, odd_w_out):
    bsz, seq, d = x.shape
    ctx_len = ctx.shape[1]
    depth = norm_g.shape[0]
    t_lat, t_all = bsz * seq, bsz * (seq + ctx_len)
    assert bsz + 1 <= MOD_ROWS

    rows = _Rows(bsz, seq, ctx_len, d)
    bm = _pick(math.gcd(t_lat, t_all), (1024, 512, 256, 128))

    c8 = jnp.concatenate([c, c_ctx[None, :], jnp.zeros((MOD_ROWS - bsz - 1, d), F32)], axis=0)
    mods = _adaln_all(c8, ada_down, ada_up, ada_b)

    a_heads, a_qk, a_v = d // 256, 64, 128
    b_heads, b_qk, b_v = d // 512, 128, 256
    cb = d // 4
    tabs = jnp.stack([_rope_tables(seq, a_qk), _rope_tables(seq, b_qk)])
    attn_rows = 256
    ret_chunk = min(256, ctx_len)
    c_heads, c_qk, c_v = d // 512, 256, 512
    gla_rank = gla_w_up.shape[2]
    gla_chunk = min(128, ctx_len)

    def even_mixer(u, e, lam_init):
        w_in = even_w_in[e].astype(BF16)
        p = _matmul(u, w_in, t_all, F32, bm, _pick(w_in.shape[1], (1024, 512, 256, 128)))
        tr = rows.tr
        n_lat_tiles = t_lat // tr
        per_batch = seq // tr
        ctx_tiles = ctx_len // tr
        table = pl.BlockSpec((1, 3, tr, LANES), lambda i, jo: (jnp.where(jo < 2, 0, 1), 0, i % per_batch, 0))
        qk = pl.pallas_call(
            functools.partial(_rope_kernel, n_lat_tiles=n_lat_tiles, rot=(a_qk // 4, a_qk // 4, b_qk // 4, b_qk // 4),
                              scales=(a_qk ** -0.5 * math.log2(math.e),) * 2 + (1.0, b_qk ** -0.5)),
            grid=(t_all // tr, 4),
            in_specs=[pl.BlockSpec((tr, cb), lambda i, jo: (i, jnp.where(jo < 2, jo, jo + 4))), table],
            out_specs=pl.BlockSpec((tr, cb), lambda i, jo: (i, jo)),
            out_shape=jax.ShapeDtypeStruct((t_all, 4 * cb), BF16),
            compiler_params=_cp(2, 32),
            name="rope_q",
        )(p, tabs)

        def kv_row(i):
            lat = (i // per_batch) * (per_batch + ctx_tiles) + ctx_tiles + i % per_batch
            j = i - n_lat_tiles
            return jnp.where(i < n_lat_tiles, lat, (j // ctx_tiles) * (per_batch + ctx_tiles) + j % ctx_tiles)

        kv = pl.pallas_call(
            functools.partial(_rope_kernel, n_lat_tiles=n_lat_tiles, rot=(a_qk // 4, a_qk // 4, 0, 0),
                              scales=(1.0,) * 4),
            grid=(t_all // tr, 4),
            in_specs=[pl.BlockSpec((tr, cb), lambda i, jo: (i, jo + 2)), table],
            out_specs=pl.BlockSpec((tr, cb), lambda i, jo: (kv_row(i), jo)),
            out_shape=jax.ShapeDtypeStruct((t_all, 4 * cb), BF16),
            compiler_params=_cp(2, 32),
            name="rope_kv",
        )(p, tabs)

        ctx_blk = t_lat // ctx_len
        hb = LANES
        va0 = 2 * cb // hb
        n_kv = seq + ctx_len
        lam_p = diff_lambda[e]
        dg = diff_g[e].reshape(1, a_v)
        small = [pl.BlockSpec((4, a_qk), lambda b, h: (0, 0)), pl.BlockSpec((1, a_v), lambda b, h: (0, 0))]
        mixed = pl.pallas_call(
            functools.partial(_attn_kernel, lam_init=lam_init, hd=a_qk, rsub=min(attn_rows, seq)),
            grid=(bsz, a_heads),
            in_specs=[pl.BlockSpec((seq, hb), lambda b, h: (b, h)),
                      pl.BlockSpec((n_kv, hb), lambda b, h: (b, h)),
                      pl.BlockSpec((n_kv, hb), lambda b, h: (b, va0 + h))] + small,
            out_specs=pl.BlockSpec((seq, hb), lambda b, h: (b, h)),
            out_shape=jax.ShapeDtypeStruct((t_all, d), BF16),
            compiler_params=_cp(2, 48),
            name="diff_attn_lat",
        )(qk, kv, kv, lam_p, dg)
        kv_ctx = n_kv // ctx_len
        mixed = pl.pallas_call(
            functools.partial(_attn_kernel, lam_init=lam_init, hd=a_qk, rsub=min(attn_rows, ctx_len)),
            grid=(bsz, a_heads),
            in_specs=[pl.BlockSpec((ctx_len, hb), lambda b, h: (ctx_blk + b, h)),
                      pl.BlockSpec((ctx_len, hb), lambda b, h: (b * kv_ctx, h)),
                      pl.BlockSpec((ctx_len, hb), lambda b, h: (b * kv_ctx, va0 + h))] + small
                     + [pl.BlockSpec(memory_space=pl.ANY)],
            out_specs=pl.BlockSpec((ctx_len, hb), lambda b, h: (ctx_blk + b, h)),
            out_shape=jax.ShapeDtypeStruct((t_all, d), BF16),
            input_output_aliases={5: 0},
            compiler_params=_cp(2, 32),
            name="diff_attn_ctx",
        )(qk, kv, kv, lam_p, dg, mixed)

        rc = ret_chunk
        qr0, kr0 = 2 * cb // b_qk, 3 * cb // b_qk
        vr0, gr0, mo0 = 8 * cb // b_v, 10 * cb // b_v, 2 * cb // b_v
        lgt = ret_logit[e].reshape(2 * b_heads, 1, 1)
        outs = None
        for reverse in (False, True):
            rmap, nsteps = _scan_rows(bsz, seq, ctx_len, rc, reverse)
            di = 1 if reverse else 0
            in_specs = [pl.BlockSpec((rc, b_qk), lambda b, h, s: (rmap(b, s), qr0 + h)),
                        pl.BlockSpec((rc, b_qk), lambda b, h, s: (rmap(b, s), kr0 + h)),
                        pl.BlockSpec((rc, b_v), lambda b, h, s: (rmap(b, s), vr0 + h)),
                        pl.BlockSpec((1, 1, 1), lambda b, h, s: (di * b_heads + h, 0, 0))]
            args = [qk, qk, p, lgt]
            if reverse:
                in_specs += [pl.BlockSpec((rc, b_v), lambda b, h, s: (rmap(b, s), h)),
                             pl.BlockSpec((rc, b_v), lambda b, h, s: (rmap(b, s), gr0 + h)),
                             pl.BlockSpec((1, 1, b_v), lambda b, h, s: (h, 0, 0)),
                             pl.BlockSpec(memory_space=pl.ANY)]
                args += [outs, p, ret_g[e].reshape(b_heads, 1, b_v), mixed]
                out_spec = pl.BlockSpec((rc, b_v), lambda b, h, s: (rmap(b, s), mo0 + h))
                out_shape = jax.ShapeDtypeStruct((t_all, d), BF16)
                aliases = {7: 0}
            else:
                out_spec = pl.BlockSpec((rc, b_v), lambda b, h, s: (rmap(b, s), h))
                out_shape = jax.ShapeDtypeStruct((t_all, b_heads * b_v), F32)
                aliases = {}
            outs = pl.pallas_call(
                functools.partial(_ret_kernel, reverse=reverse, post=reverse, chunk=rc),
                grid=(bsz, b_heads, nsteps),
                in_specs=in_specs,
                out_specs=out_spec,
                out_shape=out_shape,
                scratch_shapes=[pltpu.VMEM((b_qk, b_v), F32)],
                input_output_aliases=aliases,
                compiler_params=_cp(3, 32),
                name="retention_bwd" if reverse else "retention_fwd",
            )(*args)
        return outs, even_w_out[e].astype(BF16)

    def odd_mixer(u, o):
        w_full = odd_w_in[o]
        main_w = c_heads * (2 * c_qk + 2 * c_v)
        w_in = w_full[:, :main_w].astype(BF16)
        w_z = jnp.pad(w_full[:, main_w:], ((0, 0), (0, LANES - 2 * gla_rank))).astype(BF16)
        p = _matmul(u, w_in, t_all, F32, bm, _pick(main_w, (1024, 512, 256, 128)))
        z = _matmul(u, w_z, t_all, F32, bm, LANES)
        gc = gla_chunk
        k0, v0, r0 = c_heads, 2 * c_heads * c_qk // c_v, (2 * c_heads * c_qk + c_heads * c_v) // c_v
        outs = None
        for reverse in (False, True):
            rmap, nsteps = _scan_rows(bsz, seq, ctx_len, gc, reverse)
            di = 1 if reverse else 0
            wup = jnp.zeros((LANES, c_heads * c_qk), F32).at[di * gla_rank:(di + 1) * gla_rank].set(gla_w_up[o, di])
            bias = gla_b[o, di].reshape(1, c_heads * c_qk)
            in_specs = [pl.BlockSpec((gc, c_qk), lambda b, h, s: (rmap(b, s), h)),
                        pl.BlockSpec((gc, c_qk), lambda b, h, s: (rmap(b, s), k0 + h)),
                        pl.BlockSpec((gc, c_v), lambda b, h, s: (rmap(b, s), v0 + h)),
                        pl.BlockSpec((gc, LANES), lambda b, h, s: (rmap(b, s), 0)),
                        pl.BlockSpec((LANES, c_qk), lambda b, h, s: (0, h)),
                        pl.BlockSpec((1, c_qk), lambda b, h, s: (0, h))]
            args = [p, p, p, z, wup, bias]
            out_spec = pl.BlockSpec((gc, c_v), lambda b, h, s: (rmap(b, s), h))
            if reverse:
                in_specs += [pl.BlockSpec((gc, c_v), lambda b, h, s: (rmap(b, s), h)),
                             pl.BlockSpec((gc, c_v), lambda b, h, s: (rmap(b, s), r0 + h)),
                             pl.BlockSpec((1, 1, c_v), lambda b, h, s: (h, 0, 0))]
                args += [outs, p, gla_g[o].reshape(c_heads, 1, c_v)]
                out_shape = jax.ShapeDtypeStruct((t_all, d), BF16)
            else:
                out_shape = jax.ShapeDtypeStruct((t_all, c_heads * c_v), F32)
            outs = pl.pallas_call(
                functools.partial(_gla_kernel, reverse=reverse, post=reverse, chunk=gc, dk=c_qk),
                grid=(bsz, c_heads, nsteps),
                in_specs=in_specs,
                out_specs=out_spec,
                out_shape=out_shape,
                scratch_shapes=[pltpu.VMEM((c_v, c_qk), F32)],
                compiler_params=_cp(3, 32),
                name="gla_bwd" if reverse else "gla_fwd",
            )(*args)
        return outs, odd_w_out[o].astype(BF16)

    bn_d = _pick(d, (512, 256, 128))
    h = jnp.concatenate([x.reshape(t_lat, d), ctx.reshape(bsz * ctx_len, d)], axis=0)
    u = rows.pre(h, norm_g[0], mods[0], 0, 0)
    for l in range(depth):
        g, m = norm_g[l], mods[l]
        last = l == depth - 1
        z = _ffn_in(u, ffn_in[l, 0].astype(BF16), t_all, bm)
        y = _matmul(z, ffn_out[l, 0].astype(BF16), t_all, BF16, bm, bn_d)
        h, u = rows.post_pre(y, h, g, m, 1, 2, 0.5, g, m, 2, 3)
        if l % 2 == 0:
            mixed, w_out = even_mixer(u, l // 2, 0.8 - 0.6 * math.exp(-0.3 * l))
        else:
            mixed, w_out = odd_mixer(u, l // 2)
        n_rows = t_lat if last else t_all
        y = _matmul(mixed, w_out, n_rows, BF16, bm, bn_d)
        h, u = rows.post_pre(y, h, g, m, 3, 5, 1.0, g, m, 4, 6)
        z = _ffn_in(u, ffn_in[l, 1].astype(BF16), n_rows, bm)
        y = _matmul(z, ffn_out[l, 1].astype(BF16), n_rows, BF16, bm, bn_d)
        if last:
            h = rows.post(y, h, g, m, 5, 8, 0.5)
        else:
            h, u = rows.post_pre(y, h, g, m, 5, 8, 0.5, norm_g[l + 1], mods[l + 1], 0, 0)
    return h.reshape(bsz, seq, d)
```

```python
import functools
import math

import jax
import jax.numpy as jnp
import numpy as np
from jax import lax
from jax.experimental import pallas as pl
from jax.experimental.pallas import tpu as pltpu

F32 = jnp.float32
BF16 = jnp.bfloat16
EPS = 1e-6
GRID_W = 64
ROPE_BASE = 10000.0
GLA_TAU = 16.0
LANES = 128
MOD_ROWS = 8
N_MOD = 9
_HI = lax.Precision.HIGHEST
_NT = (((1,), (1,)), ((), ()))
_TN = (((0,), (0,)), ((), ()))


def _cp(n_axes, vmem_mb):
    return pltpu.CompilerParams(dimension_semantics=("arbitrary",) * n_axes,
                                vmem_limit_bytes=vmem_mb * 2 ** 20)


def _pick(n, cands):
    for c in cands:
        if n % c == 0:
            return c
    raise ValueError(f"no tile in {cands} divides {n}")


def _rms(x, g):
    return x * lax.rsqrt(jnp.mean(x * x, axis=-1, keepdims=True) + EPS) * g


def _silu(x):
    return x * jax.nn.sigmoid(x)


def _dot(a, b):
    return jnp.dot(a, b, preferred_element_type=F32)


def _dot_nt(a, b):
    return lax.dot_general(a, b, _NT, preferred_element_type=F32)


def _adaln_kernel(c_ref, down_ref, up_ref, b_ref, o_ref, t_ref):
    @pl.when(pl.program_id(1) == 0)
    def _():
        t_ref[...] = jnp.dot(_silu(c_ref[...]), down_ref[0], precision=_HI,
                             preferred_element_type=F32)

    o_ref[0] = jnp.dot(t_ref[...], up_ref[0], precision=_HI,
                       preferred_element_type=F32) + b_ref[0]


def _adaln_all(c8, ada_down, ada_up, ada_b):
    depth, d, r = ada_down.shape
    n = ada_up.shape[-1]
    tn = d
    out = pl.pallas_call(
        _adaln_kernel,
        grid=(depth, n // tn),
        in_specs=[
            pl.BlockSpec((MOD_ROWS, d), lambda l, j: (0, 0)),
            pl.BlockSpec((1, d, r), lambda l, j: (l, 0, 0)),
            pl.BlockSpec((1, r, tn), lambda l, j: (l, 0, j)),
            pl.BlockSpec((1, 1, tn), lambda l, j: (l, 0, j)),
        ],
        out_specs=pl.BlockSpec((1, MOD_ROWS, tn), lambda l, j: (l, 0, j)),
        out_shape=jax.ShapeDtypeStruct((depth, MOD_ROWS, n), F32),
        scratch_shapes=[pltpu.VMEM((MOD_ROWS, r), F32)],
        compiler_params=_cp(2, 40),
        name="adaln",
    )(c8, ada_down, ada_up, ada_b.reshape(depth, 1, n))
    return out.reshape(depth, MOD_ROWS, N_MOD, d)


def _modulated(h, g_ref, gi, m, j):
    return _rms(h, g_ref[gi:gi + 1, :]) * (1.0 + m[j + 1:j + 2, :]) + m[j:j + 1, :]


def _pre_kernel(h_ref, g_ref, m_ref, u_ref, *, gi, j):
    u_ref[...] = _modulated(h_ref[...], g_ref, gi, m_ref[0], j).astype(BF16)


def _post_pre_kernel(y_ref, h_ref, g_ref, m_ref, g2_ref, m2_ref, hn_ref, u_ref,
                     *, gi, jg, coef, gi2, j2):
    m = m_ref[0]
    hn = h_ref[...] + coef * (m[jg:jg + 1, :] * _rms(y_ref[...].astype(F32), g_ref[gi:gi + 1, :]))
    hn_ref[...] = hn
    u_ref[...] = _modulated(hn, g2_ref, gi2, m2_ref[0], j2).astype(BF16)


def _post_kernel(y_ref, h_ref, g_ref, m_ref, hn_ref, *, gi, jg, coef):
    m = m_ref[0]
    hn_ref[...] = h_ref[...] + coef * (m[jg:jg + 1, :] * _rms(y_ref[...].astype(F32), g_ref[gi:gi + 1, :]))


class _Rows:
    def __init__(self, bsz, seq, ctx_len, d):
        self.tr = _pick(math.gcd(seq, ctx_len), (256, 128))
        self.d = d
        per_batch = seq // self.tr
        self.row = pl.BlockSpec((self.tr, d), lambda i: (i, 0))
        self.gain = pl.BlockSpec((6, d), lambda i: (0, 0))
        self.mod = pl.BlockSpec((1, N_MOD, d), lambda i: (jnp.minimum(i // per_batch, bsz), 0, 0))

    def pre(self, h, g, m, gi, j):
        rows = h.shape[0]
        return pl.pallas_call(
            functools.partial(_pre_kernel, gi=gi, j=j),
            grid=(rows // self.tr,),
            in_specs=[self.row, self.gain, self.mod],
            out_specs=self.row,
            out_shape=jax.ShapeDtypeStruct((rows, self.d), BF16),
            compiler_params=_cp(1, 32),
            name="prenorm",
        )(h, g, m)

    def post_pre(self, y, h, g, m, gi, jg, coef, g2, m2, gi2, j2):
        rows = y.shape[0]
        return pl.pallas_call(
            functools.partial(_post_pre_kernel, gi=gi, jg=jg, coef=coef, gi2=gi2, j2=j2),
            grid=(rows // self.tr,),
            in_specs=[self.row, self.row, self.gain, self.mod, self.gain, self.mod],
            out_specs=[self.row, self.row],
            out_shape=[jax.ShapeDtypeStruct((rows, self.d), F32),
                       jax.ShapeDtypeStruct((rows, self.d), BF16)],
            compiler_params=_cp(1, 48),
            name="post_pre",
        )(y, h, g, m, g2, m2)

    def post(self, y, h, g, m, gi, jg, coef):
        rows = y.shape[0]
        return pl.pallas_call(
            functools.partial(_post_kernel, gi=gi, jg=jg, coef=coef),
            grid=(rows // self.tr,),
            in_specs=[self.row, self.row, self.gain, self.mod],
            out_specs=self.row,
            out_shape=jax.ShapeDtypeStruct((rows, self.d), F32),
            compiler_params=_cp(1, 48),
            name="post",
        )(y, h, g, m)


def _mm_kernel(x_ref, w_ref, o_ref):
    o_ref[...] = _dot(x_ref[...], w_ref[...]).astype(o_ref.dtype)


def _matmul(x, w, rows, out_dtype, bm, bn):
    k = x.shape[1]
    n = w.shape[1]
    return pl.pallas_call(
        _mm_kernel,
        grid=(rows // bm, n // bn),
        in_specs=[pl.BlockSpec((bm, k), lambda i, j: (i, 0)),
                  pl.BlockSpec((k, bn), lambda i, j: (0, j))],
        out_specs=pl.BlockSpec((bm, bn), lambda i, j: (i, j)),
        out_shape=jax.ShapeDtypeStruct((rows, n), out_dtype),
        compiler_params=_cp(2, 56),
        name="matmul",
    )(x, w)


def _ffn_in_kernel(x_ref, wg_ref, wu_ref, z_ref):
    x = x_ref[...]
    gate = _dot(x, wg_ref[...])
    up = _dot(x, wu_ref[...])
    z_ref[...] = (_silu(gate) * up).astype(BF16)


def _ffn_in(x, w_in, rows, bm):
    k = x.shape[1]
    f = w_in.shape[1] // 2
    bn = _pick(f, (512, 256, 128))
    nj = f // bn
    return pl.pallas_call(
        _ffn_in_kernel,
        grid=(rows // bm, nj),
        in_specs=[pl.BlockSpec((bm, k), lambda i, j: (i, 0)),
                  pl.BlockSpec((k, bn), lambda i, j: (0, j)),
                  pl.BlockSpec((k, bn), lambda i, j: (0, j + nj))],
        out_specs=pl.BlockSpec((bm, bn), lambda i, j: (i, j)),
        out_shape=jax.ShapeDtypeStruct((rows, f), BF16),
        compiler_params=_cp(2, 56),
        name="ffn_in",
    )(x, w_in, w_in)


def _rope_tables(seq, head_dim):
    rows = seq // GRID_W
    row = jnp.repeat(jnp.arange(rows), GRID_W).astype(F32)
    col = jnp.tile(jnp.arange(GRID_W), rows).astype(F32)
    axis_dim = head_dim // 2
    inv_freq = ROPE_BASE ** (-jnp.arange(0, axis_dim, 2, dtype=F32) / axis_dim)
    ang_r = row[:, None] * inv_freq[None, :]
    ang_c = col[:, None] * inv_freq[None, :]
    zero = jnp.zeros_like(ang_r)
    cos = jnp.concatenate([jnp.cos(ang_r), jnp.cos(ang_r), jnp.cos(ang_c), jnp.cos(ang_c)], axis=-1)
    sin_hi = jnp.concatenate([-jnp.sin(ang_r), zero, -jnp.sin(ang_c), zero], axis=-1)
    sin_lo = jnp.concatenate([zero, jnp.sin(ang_r), zero, jnp.sin(ang_c)], axis=-1)
    tab = jnp.stack([cos, sin_hi, sin_lo])
    return jnp.tile(tab, (1, 1, LANES // head_dim))


def _rope_kernel(p_ref, t_ref, o_ref, *, n_lat_tiles, rot, scales):
    i = pl.program_id(0)
    jo = pl.program_id(1)
    groups = p_ref.shape[1] // LANES
    scale = jnp.float32(scales[-1])
    for idx in range(len(scales) - 1):
        scale = jnp.where(jo == idx, jnp.float32(scales[idx]), scale)

    def plain():
        o_ref[...] = (p_ref[...] * scale).astype(BF16)

    def rotated(nf):
        cos, sin_hi, sin_lo = t_ref[0, 0], t_ref[0, 1], t_ref[0, 2]
        for g in range(groups):
            x = p_ref[:, g * LANES:(g + 1) * LANES]
            y = (x * cos + pltpu.roll(x, LANES - nf, 1) * sin_hi + pltpu.roll(x, nf, 1) * sin_lo)
            o_ref[:, g * LANES:(g + 1) * LANES] = (y * scale).astype(BF16)

    is_lat = i < n_lat_tiles
    use_plain = jnp.logical_not(is_lat)
    for nf in sorted(set(rot)):
        match = functools.reduce(jnp.logical_or, [jo == idx for idx, r in enumerate(rot) if r == nf])
        if nf == 0:
            use_plain = jnp.logical_or(use_plain, match)
        else:
            pl.when(jnp.logical_and(is_lat, match))(functools.partial(rotated, nf))
    pl.when(use_plain)(plain)


def _attn_kernel(*refs, lam_init, hd, rsub):
    q_ref, k_ref, v_ref, lam_ref, g_ref = refs[:5]
    o_ref = refs[-1]
    g = g_ref[...]
    lp = lam_ref[...]
    lam = (jnp.exp(jnp.sum(lp[0:1] * lp[1:2], axis=-1, keepdims=True))
           - jnp.exp(jnp.sum(lp[2:3] * lp[3:4], axis=-1, keepdims=True)) + lam_init)

    def body(r, carry):
        rows = pl.ds(pl.multiple_of(r * rsub, rsub), rsub)
        q = q_ref[rows, :]
        lane = lax.broadcasted_iota(jnp.int32, q.shape, 1)
        zero = jnp.zeros_like(q)
        probs = []
        for qm, w in ((jnp.where(lane < hd, q, zero), 1.0), (jnp.where(lane >= hd, q, zero), lam)):
            s = _dot_nt(qm, k_ref[...])
            p = jnp.exp2(s - jnp.max(s, axis=-1, keepdims=True))
            probs.append(p * (w / jnp.sum(p, axis=-1, keepdims=True)))
        o = _dot((probs[0] - probs[1]).astype(BF16), v_ref[...])
        o_ref[rows, :] = (_rms(o, g) * (1.0 - lam_init)).astype(BF16)
        return carry

    lax.fori_loop(0, q_ref.shape[0] // rsub, body, 0)


def _ret_kernel(*refs, reverse, post, chunk, dk, dv, heads):
    if post:
        q_ref, k_ref, v_ref, lg_ref, of_ref, gate_ref, g_ref, mix_ref, o_ref, s_ref = refs
    else:
        q_ref, k_ref, v_ref, lg_ref, o_ref, s_ref = refs

    @pl.when(pl.program_id(2) == 0)
    def _():
        s_ref[...] = jnp.zeros(s_ref.shape, F32)

    c = chunk
    row = lax.broadcasted_iota(jnp.int32, (c, 1), 0)
    col = lax.broadcasted_iota(jnp.int32, (1, c), 1)
    tq = (c - 1 - row) if reverse else row
    tk = (c - 1 - col) if reverse else col
    diff = (tq - tk).astype(F32)
    tqf = tq.astype(F32)
    for hh in range(heads):
        ks, vs = slice(hh * dk, (hh + 1) * dk), slice(hh * dv, (hh + 1) * dv)
        logit = lg_ref[hh]
        lg = jnp.minimum(logit, 0.0) - jnp.log1p(jnp.exp(-jnp.abs(logit)))
        decay = jnp.where(diff >= 0, jnp.exp(lg * jnp.maximum(diff, 0.0)), 0.0)
        q_dec = jnp.exp(lg * (tqf + 1.0))
        k_dec = jnp.exp(lg * (c - 1.0 - tqf))
        chunk_dec = jnp.exp(lg * c)

        q = q_ref[:, ks]
        k = k_ref[:, ks]
        v = v_ref[:, vs].astype(BF16)
        s = s_ref[hh]
        inner = _dot_nt(q, k) * decay
        o = _dot(inner.astype(BF16), v) + _dot((q.astype(F32) * q_dec).astype(BF16), s.astype(BF16))
        kd = (k.astype(F32) * k_dec).astype(BF16)
        s_ref[hh] = s * chunk_dec + lax.dot_general(kd, v, _TN, preferred_element_type=F32)
        if post:
            o_ref[:, vs] = (_rms(of_ref[:, vs] + o, g_ref[hh]) * _silu(gate_ref[:, vs])).astype(BF16)
        else:
            o_ref[:, vs] = o


def _gla_tables(chunk, reverse):
    nl = chunk.bit_length() - 1
    tau = np.arange(chunk)[::-1] if reverse else np.arange(chunk)
    ti, tt = tau[:, None], tau[None, :]
    blocks, masks = [], []
    for lvl in range(nl):
        same = (ti >> lvl) == (tt >> lvl)
        upper = ((ti >> lvl) & 1) == 1
        blocks.append(same & np.where(upper, tt <= ti, tt > ti))
        masks.append(((ti >> (lvl + 1)) == (tt >> (lvl + 1))) & upper & (((tt >> lvl) & 1) == 0))
    blocks += [tt <= ti, tt > ti]
    masks.append(ti == tt)
    sel = np.tile(np.concatenate(blocks, axis=0).astype(np.float32), (1, 3))
    return jnp.asarray(sel, BF16), jnp.asarray(np.stack(masks).astype(np.float32))


def _gla_kernel(*refs, post, chunk, dk, dv, heads, last_row):
    if post:
        q_ref, k_ref, v_ref, z_ref, wup_ref, bias_ref, sel_ref, mask_ref, of_ref, r_ref, g_ref, o_ref, st_ref = refs
    else:
        q_ref, k_ref, v_ref, z_ref, wup_ref, bias_ref, sel_ref, mask_ref, o_ref, st_ref = refs
        of_ref = r_ref = g_ref = None

    @pl.when(pl.program_id(2) == 0)
    def _():
        st_ref[...] = jnp.zeros(st_ref.shape, F32)

    logits = jnp.dot(z_ref[...], wup_ref[...], precision=_HI, preferred_element_type=F32) + bias_ref[...]
    a_all = (jnp.minimum(logits, 0.0) - jnp.log1p(jnp.exp(-jnp.abs(logits)))) * (1.0 / GLA_TAU)
    for hh in range(heads):
        ks, vs = slice(hh * dk, (hh + 1) * dk), slice(hh * dv, (hh + 1) * dv)
        _gla_head(q_ref[:, ks], k_ref[:, ks], v_ref[:, vs], a_all[:, ks], sel_ref, mask_ref, st_ref.at[hh],
                  None if of_ref is None else (of_ref[:, vs], r_ref[:, vs], g_ref[hh]), o_ref.at[:, vs],
                  chunk=chunk, dk=dk, last_row=last_row)


def _gla_head(q, k, v, a, sel_ref, mask_ref, st_ref, post, o_ref, *, chunk, dk, last_row):
    c = chunk
    nl = c.bit_length() - 1
    q = q * (dk ** -0.5)
    vb = v.astype(BF16)

    hi = a.astype(BF16)
    rest = a - hi.astype(F32)
    mid = rest.astype(BF16)
    lo = (rest - mid.astype(F32)).astype(BF16)
    sums = _dot(sel_ref[...], jnp.concatenate([hi, mid, lo], axis=0))
    b = sums[nl * c:(nl + 1) * c]
    to_end = sums[(nl + 1) * c:]

    intra = mask_ref[nl] * _dot_nt(q.astype(BF16), k.astype(BF16))
    for lvl in range(nl):
        e = jnp.exp(sums[lvl * c:(lvl + 1) * c])
        intra = intra + mask_ref[lvl] * _dot_nt((q * e).astype(BF16), (k * e).astype(BF16))

    st = st_ref[...]
    o = _dot(intra.astype(BF16), vb) + _dot_nt((q * jnp.exp(b)).astype(BF16), st.astype(BF16))
    k_hat = (k * jnp.exp(to_end)).astype(BF16)
    b_end = b[last_row:last_row + 1, :]
    st_ref[...] = st * jnp.exp(b_end) + lax.dot_general(vb, k_hat, _TN, preferred_element_type=F32)
    if post is not None:
        o_fwd, r, g = post
        o_ref[...] = (_rms(o_fwd + o, g) * _silu(r)).astype(BF16)
    else:
        o_ref[...] = o


def _scan_rows(bsz, seq, ctx_len, chunk, reverse):
    ncc, ncl = ctx_len // chunk, seq // chunk
    ctx_base = (bsz * seq) // chunk

    def rows(b, s):
        if reverse:
            return jnp.where(s < ncc, ctx_base + b * ncc + (ncc - 1 - s), b * ncl + (ncl - 1 - (s - ncc)))
        return jnp.where(s < ncc, ctx_base + b * ncc + s, b * ncl + (s - ncc))

    return rows, ncc + ncl


def kernel(x, c, ctx, c_ctx, norm_g, ada_down, ada_up, ada_b, ffn_in, ffn_out, even_w_in, even_w_out,
           diff_lambda, diff_g, ret_logit, ret_g, odd_w_in, gla_w_up, gla_b, gla_g, odd_w_out):
    bsz, seq, d = x.shape
    ctx_len = ctx.shape[1]
    depth = norm_g.shape[0]
    t_lat, t_all = bsz * seq, bsz * (seq + ctx_len)
    assert bsz + 1 <= MOD_ROWS

    rows = _Rows(bsz, seq, ctx_len, d)
    bm = _pick(math.gcd(t_lat, t_all), (1024, 512, 256, 128))

    c8 = jnp.concatenate([c, c_ctx[None, :], jnp.zeros((MOD_ROWS - bsz - 1, d), F32)], axis=0)
    mods = _adaln_all(c8, ada_down, ada_up, ada_b)

    a_heads, a_qk, a_v = d // 256, 64, 128
    b_heads, b_qk, b_v = d // 512, 128, 256
    cb = d // 4
    tabs = jnp.stack([_rope_tables(seq, a_qk), _rope_tables(seq, b_qk)])
    attn_rows = 256
    ret_chunk = min(256, ctx_len)
    c_heads, c_qk, c_v = d // 512, 256, 512
    gla_rank = gla_w_up.shape[2]
    gla_chunk = min(128, ctx_len)

    def even_mixer(u, e, lam_init):
        w_in = even_w_in[e].astype(BF16)
        p = _matmul(u, w_in, t_all, F32, bm, _pick(w_in.shape[1], (1024, 512, 256, 128)))
        tr = rows.tr
        n_lat_tiles = t_lat // tr
        per_batch = seq // tr
        ctx_tiles = ctx_len // tr
        table = pl.BlockSpec((1, 3, tr, LANES), lambda i, jo: (jnp.where(jo < 2, 0, 1), 0, i % per_batch, 0))
        qk = pl.pallas_call(
            functools.partial(_rope_kernel, n_lat_tiles=n_lat_tiles, rot=(a_qk // 4, a_qk // 4, b_qk // 4, b_qk // 4),
                              scales=(a_qk ** -0.5 * math.log2(math.e),) * 2 + (1.0, b_qk ** -0.5)),
            grid=(t_all // tr, 4),
            in_specs=[pl.BlockSpec((tr, cb), lambda i, jo: (i, jnp.where(jo < 2, jo, jo + 4))), table],
            out_specs=pl.BlockSpec((tr, cb), lambda i, jo: (i, jo)),
            out_shape=jax.ShapeDtypeStruct((t_all, 4 * cb), BF16),
            compiler_params=_cp(2, 32),
            name="rope_q",
        )(p, tabs)

        def kv_row(i):
            lat = (i // per_batch) * (per_batch + ctx_tiles) + ctx_tiles + i % per_batch
            j = i - n_lat_tiles
            return jnp.where(i < n_lat_tiles, lat, (j // ctx_tiles) * (per_batch + ctx_tiles) + j % ctx_tiles)

        kv = pl.pallas_call(
            functools.partial(_rope_kernel, n_lat_tiles=n_lat_tiles, rot=(a_qk // 4, a_qk // 4, 0, 0),
                              scales=(1.0,) * 4),
            grid=(t_all // tr, 4),
            in_specs=[pl.BlockSpec((tr, cb), lambda i, jo: (i, jo + 2)), table],
            out_specs=pl.BlockSpec((tr, cb), lambda i, jo: (kv_row(i), jo)),
            out_shape=jax.ShapeDtypeStruct((t_all, 4 * cb), BF16),
            compiler_params=_cp(2, 32),
            name="rope_kv",
        )(p, tabs)

        ctx_blk = t_lat // ctx_len
        hb = LANES
        va0 = 2 * cb // hb
        n_kv = seq + ctx_len
        lam_p = diff_lambda[e]
        dg = diff_g[e].reshape(1, a_v)
        small = [pl.BlockSpec((4, a_qk), lambda b, h: (0, 0)), pl.BlockSpec((1, a_v), lambda b, h: (0, 0))]
        r_lat, r_ctx = min(attn_rows, seq), min(attn_rows, ctx_len)
        mixed = pl.pallas_call(
            functools.partial(_attn_kernel, lam_init=lam_init, hd=a_qk, rsub=r_lat),
            grid=(bsz, a_heads),
            in_specs=[pl.BlockSpec((seq, hb), lambda b, h: (b, h)),
                      pl.BlockSpec((n_kv, hb), lambda b, h: (b, h)),
                      pl.BlockSpec((n_kv, hb), lambda b, h: (b, va0 + h))] + small,
            out_specs=pl.BlockSpec((seq, hb), lambda b, h: (b, h)),
            out_shape=jax.ShapeDtypeStruct((t_all, d), BF16),
            compiler_params=_cp(2, 48),
            name="diff_attn_lat",
        )(qk, kv, kv, lam_p, dg)
        kv_ctx = n_kv // ctx_len
        mixed = pl.pallas_call(
            functools.partial(_attn_kernel, lam_init=lam_init, hd=a_qk, rsub=r_ctx),
            grid=(bsz, a_heads),
            in_specs=[pl.BlockSpec((ctx_len, hb), lambda b, h: (ctx_blk + b, h)),
                      pl.BlockSpec((ctx_len, hb), lambda b, h: (b * kv_ctx, h)),
                      pl.BlockSpec((ctx_len, hb), lambda b, h: (b * kv_ctx, va0 + h))] + small
                     + [pl.BlockSpec(memory_space=pl.ANY)],
            out_specs=pl.BlockSpec((ctx_len, hb), lambda b, h: (ctx_blk + b, h)),
            out_shape=jax.ShapeDtypeStruct((t_all, d), BF16),
            input_output_aliases={5: 0},
            compiler_params=_cp(2, 32),
            name="diff_attn_ctx",
        )(qk, kv, kv, lam_p, dg, mixed)

        rc = ret_chunk
        hp = _pick(b_heads, (4, 2, 1))
        wq, wv = hp * b_qk, hp * b_v
        qr0, kr0 = 2 * cb // wq, 3 * cb // wq
        vr0, gr0, mo0 = 8 * cb // wv, 10 * cb // wv, 2 * cb // wv
        lgt = ret_logit[e].reshape(2 * b_heads, 1, 1)
        outs = None
        for reverse in (False, True):
            rmap, nsteps = _scan_rows(bsz, seq, ctx_len, rc, reverse)
            di = 1 if reverse else 0
            in_specs = [pl.BlockSpec((rc, wq), lambda b, h, s: (rmap(b, s), qr0 + h)),
                        pl.BlockSpec((rc, wq), lambda b, h, s: (rmap(b, s), kr0 + h)),
                        pl.BlockSpec((rc, wv), lambda b, h, s: (rmap(b, s), vr0 + h)),
                        pl.BlockSpec((hp, 1, 1), lambda b, h, s: (di * (b_heads // hp) + h, 0, 0))]
            args = [qk, qk, p, lgt]
            if reverse:
                in_specs += [pl.BlockSpec((rc, wv), lambda b, h, s: (rmap(b, s), h)),
                             pl.BlockSpec((rc, wv), lambda b, h, s: (rmap(b, s), gr0 + h)),
                             pl.BlockSpec((hp, 1, b_v), lambda b, h, s: (h, 0, 0)),
                             pl.BlockSpec(memory_space=pl.ANY)]
                args += [outs, p, ret_g[e].reshape(b_heads, 1, b_v), mixed]
                out_spec = pl.BlockSpec((rc, wv), lambda b, h, s: (rmap(b, s), mo0 + h))
                out_shape = jax.ShapeDtypeStruct((t_all, d), BF16)
                aliases = {7: 0}
            else:
                out_spec = pl.BlockSpec((rc, wv), lambda b, h, s: (rmap(b, s), h))
                out_shape = jax.ShapeDtypeStruct((t_all, b_heads * b_v), F32)
                aliases = {}
            outs = pl.pallas_call(
                functools.partial(_ret_kernel, reverse=reverse, post=reverse, chunk=rc, dk=b_qk, dv=b_v, heads=hp),
                grid=(bsz, b_heads // hp, nsteps),
                in_specs=in_specs,
                out_specs=out_spec,
                out_shape=out_shape,
                scratch_shapes=[pltpu.VMEM((hp, b_qk, b_v), F32)],
                input_output_aliases=aliases,
                compiler_params=_cp(3, 32),
                name="retention_bwd" if reverse else "retention_fwd",
            )(*args)
        return outs, even_w_out[e].astype(BF16)

    def odd_mixer(u, o):
        w_full = odd_w_in[o]
        main_w = c_heads * (2 * c_qk + 2 * c_v)
        w_in = w_full[:, :main_w].astype(BF16)
        w_z = jnp.pad(w_full[:, main_w:], ((0, 0), (0, LANES - 2 * gla_rank))).astype(BF16)
        p = _matmul(u, w_in, t_all, F32, bm, _pick(main_w, (1024, 512, 256, 128)))
        z = _matmul(u, w_z, t_all, F32, bm, LANES)
        gc = gla_chunk
        hp = _pick(c_heads, (4, 2, 1))
        wq, wv = hp * c_qk, hp * c_v
        k0, v0, r0 = c_heads * c_qk // wq, 2 * c_heads * c_qk // wv, (2 * c_heads * c_qk + c_heads * c_v) // wv
        outs = None
        for reverse in (False, True):
            rmap, nsteps = _scan_rows(bsz, seq, ctx_len, gc, reverse)
            di = 1 if reverse else 0
            wup = jnp.zeros((LANES, c_heads * c_qk), F32).at[di * gla_rank:(di + 1) * gla_rank].set(gla_w_up[o, di])
            bias = gla_b[o, di].reshape(1, c_heads * c_qk)
            sel, masks = _gla_tables(gc, reverse)
            in_specs = [pl.BlockSpec((gc, wq), lambda b, h, s: (rmap(b, s), h)),
                        pl.BlockSpec((gc, wq), lambda b, h, s: (rmap(b, s), k0 + h)),
                        pl.BlockSpec((gc, wv), lambda b, h, s: (rmap(b, s), v0 + h)),
                        pl.BlockSpec((gc, LANES), lambda b, h, s: (rmap(b, s), 0)),
                        pl.BlockSpec((LANES, wq), lambda b, h, s: (0, h)),
                        pl.BlockSpec((1, wq), lambda b, h, s: (0, h)),
                        pl.BlockSpec(sel.shape, lambda b, h, s: (0, 0)),
                        pl.BlockSpec(masks.shape, lambda b, h, s: (0, 0, 0))]
            args = [p, p, p, z, wup, bias, sel, masks]
            out_spec = pl.BlockSpec((gc, wv), lambda b, h, s: (rmap(b, s), h))
            if reverse:
                in_specs += [pl.BlockSpec((gc, wv), lambda b, h, s: (rmap(b, s), h)),
                             pl.BlockSpec((gc, wv), lambda b, h, s: (rmap(b, s), r0 + h)),
                             pl.BlockSpec((hp, 1, c_v), lambda b, h, s: (h, 0, 0))]
                args += [outs, p, gla_g[o].reshape(c_heads, 1, c_v)]
                out_shape = jax.ShapeDtypeStruct((t_all, d), BF16)
            else:
                out_shape = jax.ShapeDtypeStruct((t_all, c_heads * c_v), F32)
            outs = pl.pallas_call(
                functools.partial(_gla_kernel, post=reverse, chunk=gc, dk=c_qk, dv=c_v, heads=hp,
                                  last_row=0 if reverse else gc - 1),
                grid=(bsz, c_heads // hp, nsteps),
                in_specs=in_specs,
                out_specs=out_spec,
                out_shape=out_shape,
                scratch_shapes=[pltpu.VMEM((hp, c_v, c_qk), F32)],
                compiler_params=_cp(3, 40),
                name="gla_bwd" if reverse else "gla_fwd",
            )(*args)
        return outs, odd_w_out[o].astype(BF16)

    bn_d = _pick(d, (512, 256, 128))
    h = jnp.concatenate([x.reshape(t_lat, d), ctx.reshape(bsz * ctx_len, d)], axis=0)
    u = rows.pre(h, norm_g[0], mods[0], 0, 0)
    for l in range(depth):
        g, m = norm_g[l], mods[l]
        last = l == depth - 1
        z = _ffn_in(u, ffn_in[l, 0].astype(BF16), t_all, bm)
        y = _matmul(z, ffn_out[l, 0].astype(BF16), t_all, BF16, bm, bn_d)
        h, u = rows.post_pre(y, h, g, m, 1, 2, 0.5, g, m, 2, 3)
        if l % 2 == 0:
            mixed, w_out = even_mixer(u, l // 2, 0.8 - 0.6 * math.exp(-0.3 * l))
        else:
            mixed, w_out = odd_mixer(u, l // 2)
        n_rows = t_lat if last else t_all
        y = _matmul(mixed, w_out, n_rows, BF16, bm, bn_d)
        h, u = rows.post_pre(y, h, g, m, 3, 5, 1.0, g, m, 4, 6)
        z = _ffn_in(u, ffn_in[l, 1].astype(BF16), n_rows, bm)
        y = _matmul(z, ffn_out[l, 1].astype(BF16), n_rows, BF16, bm, bn_d)
        if last:
            h = rows.post(y, h, g, m, 5, 8, 0.5)
        else:
            h, u = rows.post_pre(y, h, g, m, 5, 8, 0.5, norm_g[l + 1], mods[l + 1], 0, 0)
    return h.reshape(bsz, seq, d)
```

```python
import functools
import math

import jax
import jax.numpy as jnp
import numpy as np
from jax import lax
from jax.experimental import pallas as pl
from jax.experimental.pallas import tpu as pltpu

F32 = jnp.float32
BF16 = jnp.bfloat16
EPS = 1e-6
GRID_W = 64
ROPE_BASE = 10000.0
GLA_TAU = 16.0
LANES = 128
MOD_ROWS = 8
N_MOD = 9
_HI = lax.Precision.HIGHEST
_NT = (((1,), (1,)), ((), ()))
_TN = (((0,), (0,)), ((), ()))


def _cp(n_axes, vmem_mb):
    return pltpu.CompilerParams(dimension_semantics=("arbitrary",) * n_axes,
                                vmem_limit_bytes=vmem_mb * 2 ** 20)


def _pick(n, cands):
    for c in cands:
        if n % c == 0:
            return c
    raise ValueError(f"no tile in {cands} divides {n}")


def _rms(x, g):
    return x * lax.rsqrt(jnp.mean(x * x, axis=-1, keepdims=True) + EPS) * g


def _silu(x):
    return x * jax.nn.sigmoid(x)


def _dot(a, b):
    return jnp.dot(a, b, preferred_element_type=F32)


def _dot_nt(a, b):
    return lax.dot_general(a, b, _NT, preferred_element_type=F32)


def _adaln_kernel(c_ref, down_ref, up_ref, b_ref, o_ref, t_ref):
    @pl.when(pl.program_id(1) == 0)
    def _():
        t_ref[...] = jnp.dot(_silu(c_ref[...]), down_ref[0], precision=_HI,
                             preferred_element_type=F32)

    o_ref[0] = jnp.dot(t_ref[...], up_ref[0], precision=_HI,
                       preferred_element_type=F32) + b_ref[0]


def _adaln_all(c8, ada_down, ada_up, ada_b):
    depth, d, r = ada_down.shape
    n = ada_up.shape[-1]
    tn = d
    out = pl.pallas_call(
        _adaln_kernel,
        grid=(depth, n // tn),
        in_specs=[
            pl.BlockSpec((MOD_ROWS, d), lambda l, j: (0, 0)),
            pl.BlockSpec((1, d, r), lambda l, j: (l, 0, 0)),
            pl.BlockSpec((1, r, tn), lambda l, j: (l, 0, j)),
            pl.BlockSpec((1, 1, tn), lambda l, j: (l, 0, j)),
        ],
        out_specs=pl.BlockSpec((1, MOD_ROWS, tn), lambda l, j: (l, 0, j)),
        out_shape=jax.ShapeDtypeStruct((depth, MOD_ROWS, n), F32),
        scratch_shapes=[pltpu.VMEM((MOD_ROWS, r), F32)],
        compiler_params=_cp(2, 40),
        name="adaln",
    )(c8, ada_down, ada_up, ada_b.reshape(depth, 1, n))
    return out.reshape(depth, MOD_ROWS, N_MOD, d)


def _modulated(h, g_ref, gi, m, j):
    return _rms(h, g_ref[gi:gi + 1, :]) * (1.0 + m[j + 1:j + 2, :]) + m[j:j + 1, :]


def _pre_kernel(h_ref, g_ref, m_ref, u_ref, *, gi, j):
    u_ref[...] = _modulated(h_ref[...], g_ref, gi, m_ref[0], j).astype(BF16)


def _post_pre_kernel(y_ref, h_ref, g_ref, m_ref, g2_ref, m2_ref, hn_ref, u_ref,
                     *, gi, jg, coef, gi2, j2):
    m = m_ref[0]
    hn = h_ref[...] + coef * (m[jg:jg + 1, :] * _rms(y_ref[...].astype(F32), g_ref[gi:gi + 1, :]))
    hn_ref[...] = hn
    u_ref[...] = _modulated(hn, g2_ref, gi2, m2_ref[0], j2).astype(BF16)


def _post_kernel(y_ref, h_ref, g_ref, m_ref, hn_ref, *, gi, jg, coef):
    m = m_ref[0]
    hn_ref[...] = h_ref[...] + coef * (m[jg:jg + 1, :] * _rms(y_ref[...].astype(F32), g_ref[gi:gi + 1, :]))


class _Rows:
    def __init__(self, bsz, seq, ctx_len, d):
        self.tr = _pick(math.gcd(seq, ctx_len), (256, 128))
        self.d = d
        per_batch = seq // self.tr
        self.row = pl.BlockSpec((self.tr, d), lambda i: (i, 0))
        self.gain = pl.BlockSpec((6, d), lambda i: (0, 0))
        self.mod = pl.BlockSpec((1, N_MOD, d), lambda i: (jnp.minimum(i // per_batch, bsz), 0, 0))

    def pre(self, h, g, m, gi, j):
        rows = h.shape[0]
        return pl.pallas_call(
            functools.partial(_pre_kernel, gi=gi, j=j),
            grid=(rows // self.tr,),
            in_specs=[self.row, self.gain, self.mod],
            out_specs=self.row,
            out_shape=jax.ShapeDtypeStruct((rows, self.d), BF16),
            compiler_params=_cp(1, 32),
            name="prenorm",
        )(h, g, m)

    def post_pre(self, y, h, g, m, gi, jg, coef, g2, m2, gi2, j2):
        rows = y.shape[0]
        return pl.pallas_call(
            functools.partial(_post_pre_kernel, gi=gi, jg=jg, coef=coef, gi2=gi2, j2=j2),
            grid=(rows // self.tr,),
            in_specs=[self.row, self.row, self.gain, self.mod, self.gain, self.mod],
            out_specs=[self.row, self.row],
            out_shape=[jax.ShapeDtypeStruct((rows, self.d), F32),
                       jax.ShapeDtypeStruct((rows, self.d), BF16)],
            compiler_params=_cp(1, 48),
            name="post_pre",
        )(y, h, g, m, g2, m2)

    def post(self, y, h, g, m, gi, jg, coef):
        rows = y.shape[0]
        return pl.pallas_call(
            functools.partial(_post_kernel, gi=gi, jg=jg, coef=coef),
            grid=(rows // self.tr,),
            in_specs=[self.row, self.row, self.gain, self.mod],
            out_specs=self.row,
            out_shape=jax.ShapeDtypeStruct((rows, self.d), F32),
            compiler_params=_cp(1, 48),
            name="post",
        )(y, h, g, m)


MM_VMEM_BUDGET = 45 * 2 ** 20
MM_VMEM_LIMIT_MB = 58


def _mm_tiles(rows, k, n, out_bytes, n_w):
    for bm, bn in ((1024, 512), (512, 512), (1024, 256), (512, 256), (256, 256), (256, 128), (128, 128)):
        if rows % bm or n % bn:
            continue
        planned = n_w * k * bn * (2 * 4 + 2) + 2 * bm * k * 2 + 2 * bm * bn * out_bytes + n_w * bm * bn * 4
        if planned <= MM_VMEM_BUDGET:
            return bm, bn
    raise ValueError(f"no matmul tile for rows={rows} k={k} n={n}")


def _w_spec(lead, k, bn, col0):
    return pl.BlockSpec((None,) * len(lead) + (k, bn), lambda j, i: lead + (0, col0 + j))


def _mm_kernel(x_ref, w_ref, o_ref, wb_ref):
    @pl.when(pl.program_id(1) == 0)
    def _():
        wb_ref[...] = w_ref[...].astype(BF16)

    o_ref[...] = _dot(x_ref[...], wb_ref[...]).astype(o_ref.dtype)


def _matmul(x, w, lead, n, rows, out_dtype):
    k = x.shape[1]
    bm, bn = _mm_tiles(rows, k, n, jnp.dtype(out_dtype).itemsize, 1)
    return pl.pallas_call(
        _mm_kernel,
        grid=(n // bn, rows // bm),
        in_specs=[pl.BlockSpec((bm, k), lambda j, i: (i, 0)), _w_spec(lead, k, bn, 0)],
        out_specs=pl.BlockSpec((bm, bn), lambda j, i: (i, j)),
        out_shape=jax.ShapeDtypeStruct((rows, n), out_dtype),
        scratch_shapes=[pltpu.VMEM((k, bn), BF16)],
        compiler_params=_cp(2, MM_VMEM_LIMIT_MB),
        name="matmul",
    )(x, w)


def _ffn_in_kernel(x_ref, wg_ref, wu_ref, z_ref, wb_ref):
    @pl.when(pl.program_id(1) == 0)
    def _():
        wb_ref[0] = wg_ref[...].astype(BF16)
        wb_ref[1] = wu_ref[...].astype(BF16)

    x = x_ref[...]
    gate = _dot(x, wb_ref[0])
    up = _dot(x, wb_ref[1])
    z_ref[...] = (_silu(gate) * up).astype(BF16)


def _ffn_in(x, w, lead, rows):
    k = x.shape[1]
    f = w.shape[-1] // 2
    bm, bn = _mm_tiles(rows, k, f, 2, 2)
    nj = f // bn
    return pl.pallas_call(
        _ffn_in_kernel,
        grid=(nj, rows // bm),
        in_specs=[pl.BlockSpec((bm, k), lambda j, i: (i, 0)),
                  _w_spec(lead, k, bn, 0), _w_spec(lead, k, bn, nj)],
        out_specs=pl.BlockSpec((bm, bn), lambda j, i: (i, j)),
        out_shape=jax.ShapeDtypeStruct((rows, f), BF16),
        scratch_shapes=[pltpu.VMEM((2, k, bn), BF16)],
        compiler_params=_cp(2, MM_VMEM_LIMIT_MB),
        name="ffn_in",
    )(x, w, w)


def _rope_tables(seq, head_dim):
    rows = seq // GRID_W
    row = jnp.repeat(jnp.arange(rows), GRID_W).astype(F32)
    col = jnp.tile(jnp.arange(GRID_W), rows).astype(F32)
    axis_dim = head_dim // 2
    inv_freq = ROPE_BASE ** (-jnp.arange(0, axis_dim, 2, dtype=F32) / axis_dim)
    ang_r = row[:, None] * inv_freq[None, :]
    ang_c = col[:, None] * inv_freq[None, :]
    zero = jnp.zeros_like(ang_r)
    cos = jnp.concatenate([jnp.cos(ang_r), jnp.cos(ang_r), jnp.cos(ang_c), jnp.cos(ang_c)], axis=-1)
    sin_hi = jnp.concatenate([-jnp.sin(ang_r), zero, -jnp.sin(ang_c), zero], axis=-1)
    sin_lo = jnp.concatenate([zero, jnp.sin(ang_r), zero, jnp.sin(ang_c)], axis=-1)
    tab = jnp.stack([cos, sin_hi, sin_lo])
    return jnp.tile(tab, (1, 1, LANES // head_dim))


def _rope_kernel(p_ref, t_ref, o_ref, *, n_lat_tiles, rot, scales):
    i = pl.program_id(0)
    jo = pl.program_id(1)
    groups = p_ref.shape[1] // LANES
    scale = jnp.float32(scales[-1])
    for idx in range(len(scales) - 1):
        scale = jnp.where(jo == idx, jnp.float32(scales[idx]), scale)

    def plain():
        o_ref[...] = (p_ref[...] * scale).astype(BF16)

    def rotated(nf):
        cos, sin_hi, sin_lo = t_ref[0, 0], t_ref[0, 1], t_ref[0, 2]
        for g in range(groups):
            x = p_ref[:, g * LANES:(g + 1) * LANES].astype(F32)
            y = (x * cos + pltpu.roll(x, LANES - nf, 1) * sin_hi + pltpu.roll(x, nf, 1) * sin_lo)
            o_ref[:, g * LANES:(g + 1) * LANES] = (y * scale).astype(BF16)

    is_lat = i < n_lat_tiles
    use_plain = jnp.logical_not(is_lat)
    for nf in sorted(set(rot)):
        match = functools.reduce(jnp.logical_or, [jo == idx for idx, r in enumerate(rot) if r == nf])
        if nf == 0:
            use_plain = jnp.logical_or(use_plain, match)
        else:
            pl.when(jnp.logical_and(is_lat, match))(functools.partial(rotated, nf))
    pl.when(use_plain)(plain)


def _attn_kernel(*refs, lam_init, hd, rsub):
    q_ref, k_ref, v_ref, lam_ref, g_ref = refs[:5]
    o_ref = refs[-1]
    g = g_ref[...]
    lp = lam_ref[...]
    lam = (jnp.exp(jnp.sum(lp[0:1] * lp[1:2], axis=-1, keepdims=True))
           - jnp.exp(jnp.sum(lp[2:3] * lp[3:4], axis=-1, keepdims=True)) + lam_init)

    def body(r, carry):
        rows = pl.ds(pl.multiple_of(r * rsub, rsub), rsub)
        q = q_ref[rows, :]
        lane = lax.broadcasted_iota(jnp.int32, q.shape, 1)
        zero = jnp.zeros_like(q)
        probs = []
        for qm, w in ((jnp.where(lane < hd, q, zero), 1.0), (jnp.where(lane >= hd, q, zero), lam)):
            s = _dot_nt(qm, k_ref[...])
            p = jnp.exp2(s - jnp.max(s, axis=-1, keepdims=True))
            probs.append(p * (w / jnp.sum(p, axis=-1, keepdims=True)))
        o = _dot((probs[0] - probs[1]).astype(BF16), v_ref[...])
        o_ref[rows, :] = (_rms(o, g) * (1.0 - lam_init)).astype(BF16)
        return carry

    lax.fori_loop(0, q_ref.shape[0] // rsub, body, 0)


def _ret_kernel(*refs, reverse, post, chunk, dk, dv, heads):
    if post:
        q_ref, k_ref, v_ref, lg_ref, of_ref, gate_ref, g_ref, mix_ref, o_ref, s_ref = refs
    else:
        q_ref, k_ref, v_ref, lg_ref, o_ref, s_ref = refs

    @pl.when(pl.program_id(2) == 0)
    def _():
        s_ref[...] = jnp.zeros(s_ref.shape, F32)

    c = chunk
    row = lax.broadcasted_iota(jnp.int32, (c, 1), 0)
    col = lax.broadcasted_iota(jnp.int32, (1, c), 1)
    tq = (c - 1 - row) if reverse else row
    tk = (c - 1 - col) if reverse else col
    diff = (tq - tk).astype(F32)
    tqf = tq.astype(F32)
    for hh in range(heads):
        ks, vs = slice(hh * dk, (hh + 1) * dk), slice(hh * dv, (hh + 1) * dv)
        logit = lg_ref[hh]
        lg = jnp.minimum(logit, 0.0) - jnp.log1p(jnp.exp(-jnp.abs(logit)))
        decay = jnp.where(diff >= 0, jnp.exp(lg * jnp.maximum(diff, 0.0)), 0.0)
        q_dec = jnp.exp(lg * (tqf + 1.0))
        k_dec = jnp.exp(lg * (c - 1.0 - tqf))
        chunk_dec = jnp.exp(lg * c)

        q = q_ref[:, ks]
        k = k_ref[:, ks]
        v = v_ref[:, vs].astype(BF16)
        s = s_ref[hh]
        inner = _dot_nt(q, k) * decay
        o = _dot(inner.astype(BF16), v) + _dot((q.astype(F32) * q_dec).astype(BF16), s.astype(BF16))
        kd = (k.astype(F32) * k_dec).astype(BF16)
        s_ref[hh] = s * chunk_dec + lax.dot_general(kd, v, _TN, preferred_element_type=F32)
        if post:
            gate = gate_ref[:, vs].astype(F32)
            o_ref[:, vs] = (_rms(of_ref[:, vs] + o, g_ref[hh]) * _silu(gate)).astype(BF16)
        else:
            o_ref[:, vs] = o


def _gla_tables(chunk, reverse):
    nl = chunk.bit_length() - 1
    tau = np.arange(chunk)[::-1] if reverse else np.arange(chunk)
    ti, tt = tau[:, None], tau[None, :]
    blocks, masks = [], []
    for lvl in range(nl):
        same = (ti >> lvl) == (tt >> lvl)
        upper = ((ti >> lvl) & 1) == 1
        blocks.append(same & np.where(upper, tt <= ti, tt > ti))
        masks.append(((ti >> (lvl + 1)) == (tt >> (lvl + 1))) & upper & (((tt >> lvl) & 1) == 0))
    blocks += [tt <= ti, tt > ti]
    masks.append(ti == tt)
    sel = np.tile(np.concatenate(blocks, axis=0).astype(np.float32), (1, 3))
    return jnp.asarray(sel, BF16), jnp.asarray(np.stack(masks).astype(np.float32))


def _gla_kernel(*refs, post, chunk, dk, dv, heads, last_row):
    if post:
        q_ref, k_ref, v_ref, z_ref, wup_ref, bias_ref, sel_ref, mask_ref, of_ref, r_ref, g_ref, o_ref, st_ref = refs
    else:
        q_ref, k_ref, v_ref, z_ref, wup_ref, bias_ref, sel_ref, mask_ref, o_ref, st_ref = refs
        of_ref = r_ref = g_ref = None

    @pl.when(pl.program_id(2) == 0)
    def _():
        st_ref[...] = jnp.zeros(st_ref.shape, F32)

    logits = jnp.dot(z_ref[...], wup_ref[...], precision=_HI, preferred_element_type=F32) + bias_ref[...]
    a_all = (jnp.minimum(logits, 0.0) - jnp.log1p(jnp.exp(-jnp.abs(logits)))) * (1.0 / GLA_TAU)
    for hh in range(heads):
        ks, vs = slice(hh * dk, (hh + 1) * dk), slice(hh * dv, (hh + 1) * dv)
        _gla_head(q_ref[:, ks].astype(F32), k_ref[:, ks].astype(F32), v_ref[:, vs], a_all[:, ks],
                  sel_ref, mask_ref, st_ref.at[hh],
                  None if of_ref is None else (of_ref[:, vs], r_ref[:, vs].astype(F32), g_ref[hh]), o_ref.at[:, vs],
                  chunk=chunk, dk=dk, last_row=last_row)


def _gla_head(q, k, v, a, sel_ref, mask_ref, st_ref, post, o_ref, *, chunk, dk, last_row):
    c = chunk
    nl = c.bit_length() - 1
    q = q * (dk ** -0.5)
    vb = v.astype(BF16)

    hi = a.astype(BF16)
    rest = a - hi.astype(F32)
    mid = rest.astype(BF16)
    lo = (rest - mid.astype(F32)).astype(BF16)
    sums = _dot(sel_ref[...], jnp.concatenate([hi, mid, lo], axis=0))
    b = sums[nl * c:(nl + 1) * c]
    to_end = sums[(nl + 1) * c:]

    intra = mask_ref[nl] * _dot_nt(q.astype(BF16), k.astype(BF16))
    for lvl in range(nl):
        e = jnp.exp(sums[lvl * c:(lvl + 1) * c])
        intra = intra + mask_ref[lvl] * _dot_nt((q * e).astype(BF16), (k * e).astype(BF16))

    st = st_ref[...]
    o = _dot(intra.astype(BF16), vb) + _dot_nt((q * jnp.exp(b)).astype(BF16), st.astype(BF16))
    k_hat = (k * jnp.exp(to_end)).astype(BF16)
    b_end = b[last_row:last_row + 1, :]
    st_ref[...] = st * jnp.exp(b_end) + lax.dot_general(vb, k_hat, _TN, preferred_element_type=F32)
    if post is not None:
        o_fwd, r, g = post
        o_ref[...] = (_rms(o_fwd + o, g) * _silu(r)).astype(BF16)
    else:
        o_ref[...] = o


def _scan_rows(bsz, seq, ctx_len, chunk, reverse):
    ncc, ncl = ctx_len // chunk, seq // chunk
    ctx_base = (bsz * seq) // chunk

    def rows(b, s):
        if reverse:
            return jnp.where(s < ncc, ctx_base + b * ncc + (ncc - 1 - s), b * ncl + (ncl - 1 - (s - ncc)))
        return jnp.where(s < ncc, ctx_base + b * ncc + s, b * ncl + (s - ncc))

    return rows, ncc + ncl


def kernel(x, c, ctx, c_ctx, norm_g, ada_down, ada_up, ada_b, ffn_in, ffn_out, even_w_in, even_w_out,
           diff_lambda, diff_g, ret_logit, ret_g, odd_w_in, gla_w_up, gla_b, gla_g, odd_w_out):
    bsz, seq, d = x.shape
    ctx_len = ctx.shape[1]
    depth = norm_g.shape[0]
    t_lat, t_all = bsz * seq, bsz * (seq + ctx_len)
    assert bsz + 1 <= MOD_ROWS

    rows = _Rows(bsz, seq, ctx_len, d)

    c8 = jnp.concatenate([c, c_ctx[None, :], jnp.zeros((MOD_ROWS - bsz - 1, d), F32)], axis=0)
    mods = _adaln_all(c8, ada_down, ada_up, ada_b)

    a_heads, a_qk, a_v = d // 256, 64, 128
    b_heads, b_qk, b_v = d // 512, 128, 256
    cb = d // 4
    tabs = jnp.stack([_rope_tables(seq, a_qk), _rope_tables(seq, b_qk)])
    attn_rows = 256
    ret_chunk = min(256, ctx_len)
    c_heads, c_qk, c_v = d // 512, 256, 512
    gla_rank = gla_w_up.shape[2]
    gla_chunk = min(128, ctx_len)

    def even_mixer(u, e, lam_init):
        p = _matmul(u, even_w_in, (e,), even_w_in.shape[-1], t_all, BF16)
        tr = rows.tr
        n_lat_tiles = t_lat // tr
        per_batch = seq // tr
        ctx_tiles = ctx_len // tr
        table = pl.BlockSpec((1, 3, tr, LANES), lambda i, jo: (jnp.where(jo < 2, 0, 1), 0, i % per_batch, 0))
        qk = pl.pallas_call(
            functools.partial(_rope_kernel, n_lat_tiles=n_lat_tiles, rot=(a_qk // 4, a_qk // 4, b_qk // 4, b_qk // 4),
                              scales=(a_qk ** -0.5 * math.log2(math.e),) * 2 + (1.0, b_qk ** -0.5)),
            grid=(t_all // tr, 4),
            in_specs=[pl.BlockSpec((tr, cb), lambda i, jo: (i, jnp.where(jo < 2, jo, jo + 4))), table],
            out_specs=pl.BlockSpec((tr, cb), lambda i, jo: (i, jo)),
            out_shape=jax.ShapeDtypeStruct((t_all, 4 * cb), BF16),
            compiler_params=_cp(2, 32),
            name="rope_q",
        )(p, tabs)

        def kv_row(i):
            lat = (i // per_batch) * (per_batch + ctx_tiles) + ctx_tiles + i % per_batch
            j = i - n_lat_tiles
            return jnp.where(i < n_lat_tiles, lat, (j // ctx_tiles) * (per_batch + ctx_tiles) + j % ctx_tiles)

        kv = pl.pallas_call(
            functools.partial(_rope_kernel, n_lat_tiles=n_lat_tiles, rot=(a_qk // 4, a_qk // 4, 0, 0),
                              scales=(1.0,) * 4),
            grid=(t_all // tr, 4),
            in_specs=[pl.BlockSpec((tr, cb), lambda i, jo: (i, jo + 2)), table],
            out_specs=pl.BlockSpec((tr, cb), lambda i, jo: (kv_row(i), jo)),
            out_shape=jax.ShapeDtypeStruct((t_all, 4 * cb), BF16),
            compiler_params=_cp(2, 32),
            name="rope_kv",
        )(p, tabs)

        ctx_blk = t_lat // ctx_len
        hb = LANES
        va0 = 2 * cb // hb
        n_kv = seq + ctx_len
        lam_p = diff_lambda[e]
        dg = diff_g[e].reshape(1, a_v)
        small = [pl.BlockSpec((4, a_qk), lambda b, h: (0, 0)), pl.BlockSpec((1, a_v), lambda b, h: (0, 0))]
        r_lat, r_ctx = min(attn_rows, seq), min(attn_rows, ctx_len)
        mixed = pl.pallas_call(
            functools.partial(_attn_kernel, lam_init=lam_init, hd=a_qk, rsub=r_lat),
            grid=(bsz, a_heads),
            in_specs=[pl.BlockSpec((seq, hb), lambda b, h: (b, h)),
                      pl.BlockSpec((n_kv, hb), lambda b, h: (b, h)),
                      pl.BlockSpec((n_kv, hb), lambda b, h: (b, va0 + h))] + small,
            out_specs=pl.BlockSpec((seq, hb), lambda b, h: (b, h)),
            out_shape=jax.ShapeDtypeStruct((t_all, d), BF16),
            compiler_params=_cp(2, 48),
            name="diff_attn_lat",
        )(qk, kv, kv, lam_p, dg)
        kv_ctx = n_kv // ctx_len
        mixed = pl.pallas_call(
            functools.partial(_attn_kernel, lam_init=lam_init, hd=a_qk, rsub=r_ctx),
            grid=(bsz, a_heads),
            in_specs=[pl.BlockSpec((ctx_len, hb), lambda b, h: (ctx_blk + b, h)),
                      pl.BlockSpec((ctx_len, hb), lambda b, h: (b * kv_ctx, h)),
                      pl.BlockSpec((ctx_len, hb), lambda b, h: (b * kv_ctx, va0 + h))] + small
                     + [pl.BlockSpec(memory_space=pl.ANY)],
            out_specs=pl.BlockSpec((ctx_len, hb), lambda b, h: (ctx_blk + b, h)),
            out_shape=jax.ShapeDtypeStruct((t_all, d), BF16),
            input_output_aliases={5: 0},
            compiler_params=_cp(2, 32),
            name="diff_attn_ctx",
        )(qk, kv, kv, lam_p, dg, mixed)

        rc = ret_chunk
        hp = _pick(b_heads, (4, 2, 1))
        wq, wv = hp * b_qk, hp * b_v
        qr0, kr0 = 2 * cb // wq, 3 * cb // wq
        vr0, gr0, mo0 = 8 * cb // wv, 10 * cb // wv, 2 * cb // wv
        lgt = ret_logit[e].reshape(2 * b_heads, 1, 1)
        outs = None
        for reverse in (False, True):
            rmap, nsteps = _scan_rows(bsz, seq, ctx_len, rc, reverse)
            di = 1 if reverse else 0
            in_specs = [pl.BlockSpec((rc, wq), lambda b, h, s: (rmap(b, s), qr0 + h)),
                        pl.BlockSpec((rc, wq), lambda b, h, s: (rmap(b, s), kr0 + h)),
                        pl.BlockSpec((rc, wv), lambda b, h, s: (rmap(b, s), vr0 + h)),
                        pl.BlockSpec((hp, 1, 1), lambda b, h, s: (di * (b_heads // hp) + h, 0, 0))]
            args = [qk, qk, p, lgt]
            if reverse:
                in_specs += [pl.BlockSpec((rc, wv), lambda b, h, s: (rmap(b, s), h)),
                             pl.BlockSpec((rc, wv), lambda b, h, s: (rmap(b, s), gr0 + h)),
                             pl.BlockSpec((hp, 1, b_v), lambda b, h, s: (h, 0, 0)),
                             pl.BlockSpec(memory_space=pl.ANY)]
                args += [outs, p, ret_g[e].reshape(b_heads, 1, b_v), mixed]
                out_spec = pl.BlockSpec((rc, wv), lambda b, h, s: (rmap(b, s), mo0 + h))
                out_shape = jax.ShapeDtypeStruct((t_all, d), BF16)
                aliases = {7: 0}
            else:
                out_spec = pl.BlockSpec((rc, wv), lambda b, h, s: (rmap(b, s), h))
                out_shape = jax.ShapeDtypeStruct((t_all, b_heads * b_v), F32)
                aliases = {}
            outs = pl.pallas_call(
                functools.partial(_ret_kernel, reverse=reverse, post=reverse, chunk=rc, dk=b_qk, dv=b_v, heads=hp),
                grid=(bsz, b_heads // hp, nsteps),
                in_specs=in_specs,
                out_specs=out_spec,
                out_shape=out_shape,
                scratch_shapes=[pltpu.VMEM((hp, b_qk, b_v), F32)],
                input_output_aliases=aliases,
                compiler_params=_cp(3, 32),
                name="retention_bwd" if reverse else "retention_fwd",
            )(*args)
        return outs, even_w_out, (e,)

    def odd_mixer(u, o):
        w_full = odd_w_in[o]
        main_w = c_heads * (2 * c_qk + 2 * c_v)
        w_z = jnp.pad(w_full[:, main_w:], ((0, 0), (0, LANES - 2 * gla_rank)))
        p = _matmul(u, odd_w_in, (o,), main_w, t_all, BF16)
        z = _matmul(u, w_z, (), LANES, t_all, F32)
        gc = gla_chunk
        hp = _pick(c_heads, (4, 2, 1))
        wq, wv = hp * c_qk, hp * c_v
        k0, v0, r0 = c_heads * c_qk // wq, 2 * c_heads * c_qk // wv, (2 * c_heads * c_qk + c_heads * c_v) // wv
        outs = None
        for reverse in (False, True):
            rmap, nsteps = _scan_rows(bsz, seq, ctx_len, gc, reverse)
            di = 1 if reverse else 0
            wup = jnp.zeros((LANES, c_heads * c_qk), F32).at[di * gla_rank:(di + 1) * gla_rank].set(gla_w_up[o, di])
            bias = gla_b[o, di].reshape(1, c_heads * c_qk)
            sel, masks = _gla_tables(gc, reverse)
            in_specs = [pl.BlockSpec((gc, wq), lambda b, h, s: (rmap(b, s), h)),
                        pl.BlockSpec((gc, wq), lambda b, h, s: (rmap(b, s), k0 + h)),
                        pl.BlockSpec((gc, wv), lambda b, h, s: (rmap(b, s), v0 + h)),
                        pl.BlockSpec((gc, LANES), lambda b, h, s: (rmap(b, s), 0)),
                        pl.BlockSpec((LANES, wq), lambda b, h, s: (0, h)),
                        pl.BlockSpec((1, wq), lambda b, h, s: (0, h)),
                        pl.BlockSpec(sel.shape, lambda b, h, s: (0, 0)),
                        pl.BlockSpec(masks.shape, lambda b, h, s: (0, 0, 0))]
            args = [p, p, p, z, wup, bias, sel, masks]
            out_spec = pl.BlockSpec((gc, wv), lambda b, h, s: (rmap(b, s), h))
            if reverse:
                in_specs += [pl.BlockSpec((gc, wv), lambda b, h, s: (rmap(b, s), h)),
                             pl.BlockSpec((gc, wv), lambda b, h, s: (rmap(b, s), r0 + h)),
                             pl.BlockSpec((hp, 1, c_v), lambda b, h, s: (h, 0, 0))]
                args += [outs, p, gla_g[o].reshape(c_heads, 1, c_v)]
                out_shape = jax.ShapeDtypeStruct((t_all, d), BF16)
            else:
                out_shape = jax.ShapeDtypeStruct((t_all, c_heads * c_v), F32)
            outs = pl.pallas_call(
                functools.partial(_gla_kernel, post=reverse, chunk=gc, dk=c_qk, dv=c_v, heads=hp,
                                  last_row=0 if reverse else gc - 1),
                grid=(bsz, c_heads // hp, nsteps),
                in_specs=in_specs,
                out_specs=out_spec,
                out_shape=out_shape,
                scratch_shapes=[pltpu.VMEM((hp, c_v, c_qk), F32)],
                compiler_params=_cp(3, 40),
                name="gla_bwd" if reverse else "gla_fwd",
            )(*args)
        return outs, odd_w_out, (o,)

    h = jnp.concatenate([x.reshape(t_lat, d), ctx.reshape(bsz * ctx_len, d)], axis=0)
    u = rows.pre(h, norm_g[0], mods[0], 0, 0)
    for l in range(depth):
        g, m = norm_g[l], mods[l]
        last = l == depth - 1
        z = _ffn_in(u, ffn_in, (l, 0), t_all)
        y = _matmul(z, ffn_out, (l, 0), d, t_all, BF16)
        h, u = rows.post_pre(y, h, g, m, 1, 2, 0.5, g, m, 2, 3)
        if l % 2 == 0:
            mixed, w_out, lead = even_mixer(u, l // 2, 0.8 - 0.6 * math.exp(-0.3 * l))
        else:
            mixed, w_out, lead = odd_mixer(u, l // 2)
        n_rows = t_lat if last else t_all
        y = _matmul(mixed, w_out, lead, d, n_rows, BF16)
        h, u = rows.post_pre(y, h, g, m, 3, 5, 1.0, g, m, 4, 6)
        z = _ffn_in(u, ffn_in, (l, 1), n_rows)
        y = _matmul(z, ffn_out, (l, 1), d, n_rows, BF16)
        if last:
            h = rows.post(y, h, g, m, 5, 8, 0.5)
        else:
            h, u = rows.post_pre(y, h, g, m, 5, 8, 0.5, norm_g[l + 1], mods[l + 1], 0, 0)
    return h.reshape(bsz, seq, d)
```

```python
import functools
import math

import jax
import jax.numpy as jnp
import numpy as np
from jax import lax
from jax.experimental import pallas as pl
from jax.experimental.pallas import tpu as pltpu

F32 = jnp.float32
BF16 = jnp.bfloat16
EPS = 1e-6
GRID_W = 64
ROPE_BASE = 10000.0
GLA_TAU = 16.0
LANES = 128
MOD_ROWS = 8
N_MOD = 9
_HI = lax.Precision.HIGHEST
_NT = (((1,), (1,)), ((), ()))
_TN = (((0,), (0,)), ((), ()))


def _cp(n_axes, vmem_mb):
    return pltpu.CompilerParams(dimension_semantics=("arbitrary",) * n_axes,
                                vmem_limit_bytes=vmem_mb * 2 ** 20)


def _pick(n, cands):
    for c in cands:
        if n % c == 0:
            return c
    raise ValueError(f"no tile in {cands} divides {n}")


def _rms(x, g):
    return x * lax.rsqrt(jnp.mean(x * x, axis=-1, keepdims=True) + EPS) * g


def _silu(x):
    return x * jax.nn.sigmoid(x)


def _dot(a, b):
    return jnp.dot(a, b, preferred_element_type=F32)


def _dot_nt(a, b):
    return lax.dot_general(a, b, _NT, preferred_element_type=F32)


def _adaln_kernel(c_ref, down_ref, up_ref, b_ref, o_ref, t_ref):
    @pl.when(pl.program_id(1) == 0)
    def _():
        t_ref[...] = jnp.dot(_silu(c_ref[...]), down_ref[0], precision=_HI,
                             preferred_element_type=F32)

    o_ref[0] = jnp.dot(t_ref[...], up_ref[0], precision=_HI,
                       preferred_element_type=F32) + b_ref[0]


def _adaln_all(c8, ada_down, ada_up, ada_b):
    depth, d, r = ada_down.shape
    n = ada_up.shape[-1]
    tn = d
    out = pl.pallas_call(
        _adaln_kernel,
        grid=(depth, n // tn),
        in_specs=[
            pl.BlockSpec((MOD_ROWS, d), lambda l, j: (0, 0)),
            pl.BlockSpec((1, d, r), lambda l, j: (l, 0, 0)),
            pl.BlockSpec((1, r, tn), lambda l, j: (l, 0, j)),
            pl.BlockSpec((1, 1, tn), lambda l, j: (l, 0, j)),
        ],
        out_specs=pl.BlockSpec((1, MOD_ROWS, tn), lambda l, j: (l, 0, j)),
        out_shape=jax.ShapeDtypeStruct((depth, MOD_ROWS, n), F32),
        scratch_shapes=[pltpu.VMEM((MOD_ROWS, r), F32)],
        compiler_params=_cp(2, 40),
        name="adaln",
    )(c8, ada_down, ada_up, ada_b.reshape(depth, 1, n))
    return out.reshape(depth, MOD_ROWS, N_MOD, d)


def _modulated(h, g_ref, gi, m, j):
    return _rms(h, g_ref[gi:gi + 1, :]) * (1.0 + m[j + 1:j + 2, :]) + m[j:j + 1, :]


def _pre_kernel(h_ref, g_ref, m_ref, u_ref, *, gi, j):
    u_ref[...] = _modulated(h_ref[...], g_ref, gi, m_ref[0], j).astype(BF16)


def _post_pre_kernel(y_ref, h_ref, g_ref, m_ref, g2_ref, m2_ref, hn_ref, u_ref,
                     *, gi, jg, coef, gi2, j2):
    m = m_ref[0]
    hn = h_ref[...] + coef * (m[jg:jg + 1, :] * _rms(y_ref[...].astype(F32), g_ref[gi:gi + 1, :]))
    hn_ref[...] = hn
    u_ref[...] = _modulated(hn, g2_ref, gi2, m2_ref[0], j2).astype(BF16)


def _post_kernel(y_ref, h_ref, g_ref, m_ref, hn_ref, *, gi, jg, coef):
    m = m_ref[0]
    hn_ref[...] = h_ref[...] + coef * (m[jg:jg + 1, :] * _rms(y_ref[...].astype(F32), g_ref[gi:gi + 1, :]))


class _Rows:
    def __init__(self, bsz, seq, ctx_len, d):
        self.tr = _pick(math.gcd(seq, ctx_len), (256, 128))
        self.d = d
        per_batch = seq // self.tr
        self.row = pl.BlockSpec((self.tr, d), lambda i: (i, 0))
        self.gain = pl.BlockSpec((6, d), lambda i: (0, 0))
        self.mod = pl.BlockSpec((1, N_MOD, d), lambda i: (jnp.minimum(i // per_batch, bsz), 0, 0))

    def pre(self, h, g, m, gi, j):
        rows = h.shape[0]
        return pl.pallas_call(
            functools.partial(_pre_kernel, gi=gi, j=j),
            grid=(rows // self.tr,),
            in_specs=[self.row, self.gain, self.mod],
            out_specs=self.row,
            out_shape=jax.ShapeDtypeStruct((rows, self.d), BF16),
            compiler_params=_cp(1, 32),
            name="prenorm",
        )(h, g, m)

    def post_pre(self, y, h, g, m, gi, jg, coef, g2, m2, gi2, j2):
        rows = y.shape[0]
        return pl.pallas_call(
            functools.partial(_post_pre_kernel, gi=gi, jg=jg, coef=coef, gi2=gi2, j2=j2),
            grid=(rows // self.tr,),
            in_specs=[self.row, self.row, self.gain, self.mod, self.gain, self.mod],
            out_specs=[self.row, self.row],
            out_shape=[jax.ShapeDtypeStruct((rows, self.d), F32),
                       jax.ShapeDtypeStruct((rows, self.d), BF16)],
            compiler_params=_cp(1, 48),
            name="post_pre",
        )(y, h, g, m, g2, m2)

    def post(self, y, h, g, m, gi, jg, coef):
        rows = y.shape[0]
        return pl.pallas_call(
            functools.partial(_post_kernel, gi=gi, jg=jg, coef=coef),
            grid=(rows // self.tr,),
            in_specs=[self.row, self.row, self.gain, self.mod],
            out_specs=self.row,
            out_shape=jax.ShapeDtypeStruct((rows, self.d), F32),
            compiler_params=_cp(1, 48),
            name="post",
        )(y, h, g, m)


MM_VMEM_BUDGET = 44 * 2 ** 20
MM_VMEM_LIMIT_MB = 58
BF16_SUBLANES = 16


def _mm_tiles(rows, k, n, out_bytes, n_w):
    bm = _pick(rows, (1024, 512, 256, 128))
    for bn in (1024, 512, 256, 128):
        planned = 2 * bm * k * 2 + 2 * n_w * k * bn * 2 + 2 * bm * bn * out_bytes + n_w * bm * bn * 4
        if n % bn == 0 and planned <= MM_VMEM_BUDGET:
            return bm, bn
    raise ValueError(f"no matmul tile for rows={rows} k={k} n={n}")


def _side_job(src, lead, ni, nj):
    k, n = src.shape[-2:]
    slab = next((s for s in (BF16_SUBLANES * 2 ** p for p in range(8)) if k % s == 0 and k // s <= ni * nj), None)
    if slab is None:
        return None
    last = k // slab - 1
    return dict(
        in_spec=pl.BlockSpec((None,) * len(lead) + (slab, n), lambda i, j: lead + (jnp.minimum(i * nj + j, last), 0)),
        out_spec=pl.BlockSpec((slab, n), lambda i, j: (jnp.minimum(i * nj + j, last), 0)),
        out_shape=jax.ShapeDtypeStruct((k, n), BF16))


def _mm_call(body, grid, operands, in_specs, out_spec, out_shape, nxt, name):
    job = _side_job(*nxt, *grid) if nxt is not None else None
    params = dict(grid=grid, compiler_params=_cp(2, MM_VMEM_LIMIT_MB), name=name)
    if job is None:
        out = pl.pallas_call(functools.partial(body, side=False), in_specs=in_specs, out_specs=out_spec,
                             out_shape=out_shape, **params)(*operands)
        return out, (None if nxt is None else nxt[0][nxt[1]].astype(BF16))
    return pl.pallas_call(functools.partial(body, side=True), in_specs=in_specs + [job["in_spec"]],
                          out_specs=[out_spec, job["out_spec"]], out_shape=[out_shape, job["out_shape"]],
                          **params)(*operands, nxt[0])


def _mm_kernel(*refs, side):
    if side:
        x_ref, w_ref, nxt_ref, o_ref, nxt_o_ref = refs
        nxt_o_ref[...] = nxt_ref[...].astype(BF16)
    else:
        x_ref, w_ref, o_ref = refs
    o_ref[...] = _dot(x_ref[...], w_ref[...]).astype(o_ref.dtype)


def _matmul(x, w, n, rows, out_dtype, nxt=None):
    k = x.shape[1]
    bm, bn = _mm_tiles(rows, k, n, jnp.dtype(out_dtype).itemsize, 1)
    return _mm_call(_mm_kernel, (rows // bm, n // bn), (x, w),
                    [pl.BlockSpec((bm, k), lambda i, j: (i, 0)), pl.BlockSpec((k, bn), lambda i, j: (0, j))],
                    pl.BlockSpec((bm, bn), lambda i, j: (i, j)), jax.ShapeDtypeStruct((rows, n), out_dtype),
                    nxt, "matmul")


def _ffn_in_kernel(*refs, side):
    if side:
        x_ref, wg_ref, wu_ref, nxt_ref, z_ref, nxt_o_ref = refs
        nxt_o_ref[...] = nxt_ref[...].astype(BF16)
    else:
        x_ref, wg_ref, wu_ref, z_ref = refs
    x = x_ref[...]
    gate = _dot(x, wg_ref[...])
    up = _dot(x, wu_ref[...])
    z_ref[...] = (_silu(gate) * up).astype(BF16)


def _ffn_in(x, w, rows, nxt=None):
    k = x.shape[1]
    f = w.shape[1] // 2
    bm, bn = _mm_tiles(rows, k, f, 2, 2)
    nj = f // bn
    return _mm_call(_ffn_in_kernel, (rows // bm, nj), (x, w, w),
                    [pl.BlockSpec((bm, k), lambda i, j: (i, 0)), pl.BlockSpec((k, bn), lambda i, j: (0, j)),
                     pl.BlockSpec((k, bn), lambda i, j: (0, j + nj))],
                    pl.BlockSpec((bm, bn), lambda i, j: (i, j)), jax.ShapeDtypeStruct((rows, f), BF16),
                    nxt, "ffn_in")


def _rope_tables(seq, head_dim):
    rows = seq // GRID_W
    row = jnp.repeat(jnp.arange(rows), GRID_W).astype(F32)
    col = jnp.tile(jnp.arange(GRID_W), rows).astype(F32)
    axis_dim = head_dim // 2
    inv_freq = ROPE_BASE ** (-jnp.arange(0, axis_dim, 2, dtype=F32) / axis_dim)
    ang_r = row[:, None] * inv_freq[None, :]
    ang_c = col[:, None] * inv_freq[None, :]
    zero = jnp.zeros_like(ang_r)
    cos = jnp.concatenate([jnp.cos(ang_r), jnp.cos(ang_r), jnp.cos(ang_c), jnp.cos(ang_c)], axis=-1)
    sin_hi = jnp.concatenate([-jnp.sin(ang_r), zero, -jnp.sin(ang_c), zero], axis=-1)
    sin_lo = jnp.concatenate([zero, jnp.sin(ang_r), zero, jnp.sin(ang_c)], axis=-1)
    tab = jnp.stack([cos, sin_hi, sin_lo])
    return jnp.tile(tab, (1, 1, LANES // head_dim))


def _rope_kernel(p_ref, t_ref, o_ref, *, n_lat_tiles, rot, scales):
    i = pl.program_id(0)
    jo = pl.program_id(1)
    groups = p_ref.shape[1] // LANES
    scale = jnp.float32(scales[-1])
    for idx in range(len(scales) - 1):
        scale = jnp.where(jo == idx, jnp.float32(scales[idx]), scale)

    def plain():
        o_ref[...] = (p_ref[...] * scale).astype(BF16)

    def rotated(nf):
        cos, sin_hi, sin_lo = t_ref[0, 0], t_ref[0, 1], t_ref[0, 2]
        for g in range(groups):
            x = p_ref[:, g * LANES:(g + 1) * LANES].astype(F32)
            y = (x * cos + pltpu.roll(x, LANES - nf, 1) * sin_hi + pltpu.roll(x, nf, 1) * sin_lo)
            o_ref[:, g * LANES:(g + 1) * LANES] = (y * scale).astype(BF16)

    is_lat = i < n_lat_tiles
    use_plain = jnp.logical_not(is_lat)
    for nf in sorted(set(rot)):
        match = functools.reduce(jnp.logical_or, [jo == idx for idx, r in enumerate(rot) if r == nf])
        if nf == 0:
            use_plain = jnp.logical_or(use_plain, match)
        else:
            pl.when(jnp.logical_and(is_lat, match))(functools.partial(rotated, nf))
    pl.when(use_plain)(plain)


def _attn_kernel(*refs, lam_init, hd, rsub):
    q_ref, k_ref, v_ref, lam_ref, g_ref = refs[:5]
    o_ref = refs[-1]
    g = g_ref[...]
    lp = lam_ref[...]
    lam = (jnp.exp(jnp.sum(lp[0:1] * lp[1:2], axis=-1, keepdims=True))
           - jnp.exp(jnp.sum(lp[2:3] * lp[3:4], axis=-1, keepdims=True)) + lam_init)

    def body(r, carry):
        rows = pl.ds(pl.multiple_of(r * rsub, rsub), rsub)
        q = q_ref[rows, :]
        lane = lax.broadcasted_iota(jnp.int32, q.shape, 1)
        zero = jnp.zeros_like(q)
        probs = []
        for qm, w in ((jnp.where(lane < hd, q, zero), 1.0), (jnp.where(lane >= hd, q, zero), lam)):
            s = _dot_nt(qm, k_ref[...])
            p = jnp.exp2(s - jnp.max(s, axis=-1, keepdims=True))
            probs.append(p * (w / jnp.sum(p, axis=-1, keepdims=True)))
        o = _dot((probs[0] - probs[1]).astype(BF16), v_ref[...])
        o_ref[rows, :] = (_rms(o, g) * (1.0 - lam_init)).astype(BF16)
        return carry

    lax.fori_loop(0, q_ref.shape[0] // rsub, body, 0)


def _ret_kernel(*refs, reverse, post, chunk, dk, dv, heads):
    if post:
        q_ref, k_ref, v_ref, lg_ref, of_ref, gate_ref, g_ref, mix_ref, o_ref, s_ref = refs
    else:
        q_ref, k_ref, v_ref, lg_ref, o_ref, s_ref = refs

    @pl.when(pl.program_id(2) == 0)
    def _():
        s_ref[...] = jnp.zeros(s_ref.shape, F32)

    c = chunk
    row = lax.broadcasted_iota(jnp.int32, (c, 1), 0)
    col = lax.broadcasted_iota(jnp.int32, (1, c), 1)
    tq = (c - 1 - row) if reverse else row
    tk = (c - 1 - col) if reverse else col
    diff = (tq - tk).astype(F32)
    tqf = tq.astype(F32)
    for hh in range(heads):
        ks, vs = slice(hh * dk, (hh + 1) * dk), slice(hh * dv, (hh + 1) * dv)
        logit = lg_ref[hh]
        lg = jnp.minimum(logit, 0.0) - jnp.log1p(jnp.exp(-jnp.abs(logit)))
        decay = jnp.where(diff >= 0, jnp.exp(lg * jnp.maximum(diff, 0.0)), 0.0)
        q_dec = jnp.exp(lg * (tqf + 1.0))
        k_dec = jnp.exp(lg * (c - 1.0 - tqf))
        chunk_dec = jnp.exp(lg * c)

        q = q_ref[:, ks]
        k = k_ref[:, ks]
        v = v_ref[:, vs].astype(BF16)
        s = s_ref[hh]
        inner = _dot_nt(q, k) * decay
        o = _dot(inner.astype(BF16), v) + _dot((q.astype(F32) * q_dec).astype(BF16), s.astype(BF16))
        kd = (k.astype(F32) * k_dec).astype(BF16)
        s_ref[hh] = s * chunk_dec + lax.dot_general(kd, v, _TN, preferred_element_type=F32)
        if post:
            gate = gate_ref[:, vs].astype(F32)
            o_ref[:, vs] = (_rms(of_ref[:, vs] + o, g_ref[hh]) * _silu(gate)).astype(BF16)
        else:
            o_ref[:, vs] = o


def _gla_tables(chunk, reverse):
    nl = chunk.bit_length() - 1
    tau = np.arange(chunk)[::-1] if reverse else np.arange(chunk)
    ti, tt = tau[:, None], tau[None, :]
    blocks, masks = [], []
    for lvl in range(nl):
        same = (ti >> lvl) == (tt >> lvl)
        upper = ((ti >> lvl) & 1) == 1
        blocks.append(same & np.where(upper, tt <= ti, tt > ti))
        masks.append(((ti >> (lvl + 1)) == (tt >> (lvl + 1))) & upper & (((tt >> lvl) & 1) == 0))
    blocks += [tt <= ti, tt > ti]
    masks.append(ti == tt)
    sel = np.tile(np.concatenate(blocks, axis=0).astype(np.float32), (1, 3))
    return jnp.asarray(sel, BF16), jnp.asarray(np.stack(masks).astype(np.float32))


def _gla_kernel(*refs, post, chunk, dk, dv, heads, last_row):
    if post:
        q_ref, k_ref, v_ref, z_ref, wup_ref, bias_ref, sel_ref, mask_ref, of_ref, r_ref, g_ref, o_ref, st_ref = refs
    else:
        q_ref, k_ref, v_ref, z_ref, wup_ref, bias_ref, sel_ref, mask_ref, o_ref, st_ref = refs
        of_ref = r_ref = g_ref = None

    @pl.when(pl.program_id(2) == 0)
    def _():
        st_ref[...] = jnp.zeros(st_ref.shape, F32)

    logits = jnp.dot(z_ref[...], wup_ref[...], precision=_HI, preferred_element_type=F32) + bias_ref[...]
    a_all = (jnp.minimum(logits, 0.0) - jnp.log1p(jnp.exp(-jnp.abs(logits)))) * (1.0 / GLA_TAU)
    for hh in range(heads):
        ks, vs = slice(hh * dk, (hh + 1) * dk), slice(hh * dv, (hh + 1) * dv)
        _gla_head(q_ref[:, ks].astype(F32), k_ref[:, ks].astype(F32), v_ref[:, vs], a_all[:, ks],
                  sel_ref, mask_ref, st_ref.at[hh],
                  None if of_ref is None else (of_ref[:, vs], r_ref[:, vs].astype(F32), g_ref[hh]), o_ref.at[:, vs],
                  chunk=chunk, dk=dk, last_row=last_row)


def _gla_head(q, k, v, a, sel_ref, mask_ref, st_ref, post, o_ref, *, chunk, dk, last_row):
    c = chunk
    nl = c.bit_length() - 1
    q = q * (dk ** -0.5)
    vb = v.astype(BF16)

    hi = a.astype(BF16)
    rest = a - hi.astype(F32)
    mid = rest.astype(BF16)
    lo = (rest - mid.astype(F32)).astype(BF16)
    sums = _dot(sel_ref[...], jnp.concatenate([hi, mid, lo], axis=0))
    b = sums[nl * c:(nl + 1) * c]
    to_end = sums[(nl + 1) * c:]

    intra = mask_ref[nl] * _dot_nt(q.astype(BF16), k.astype(BF16))
    for lvl in range(nl):
        e = jnp.exp(sums[lvl * c:(lvl + 1) * c])
        intra = intra + mask_ref[lvl] * _dot_nt((q * e).astype(BF16), (k * e).astype(BF16))

    st = st_ref[...]
    o = _dot(intra.astype(BF16), vb) + _dot_nt((q * jnp.exp(b)).astype(BF16), st.astype(BF16))
    k_hat = (k * jnp.exp(to_end)).astype(BF16)
    b_end = b[last_row:last_row + 1, :]
    st_ref[...] = st * jnp.exp(b_end) + lax.dot_general(vb, k_hat, _TN, preferred_element_type=F32)
    if post is not None:
        o_fwd, r, g = post
        o_ref[...] = (_rms(o_fwd + o, g) * _silu(r)).astype(BF16)
    else:
        o_ref[...] = o


def _scan_rows(bsz, seq, ctx_len, chunk, reverse):
    ncc, ncl = ctx_len // chunk, seq // chunk
    ctx_base = (bsz * seq) // chunk

    def rows(b, s):
        if reverse:
            return jnp.where(s < ncc, ctx_base + b * ncc + (ncc - 1 - s), b * ncl + (ncl - 1 - (s - ncc)))
        return jnp.where(s < ncc, ctx_base + b * ncc + s, b * ncl + (s - ncc))

    return rows, ncc + ncl


def kernel(x, c, ctx, c_ctx, norm_g, ada_down, ada_up, ada_b, ffn_in, ffn_out, even_w_in, even_w_out,
           diff_lambda, diff_g, ret_logit, ret_g, odd_w_in, gla_w_up, gla_b, gla_g, odd_w_out):
    bsz, seq, d = x.shape
    ctx_len = ctx.shape[1]
    depth = norm_g.shape[0]
    t_lat, t_all = bsz * seq, bsz * (seq + ctx_len)
    assert bsz + 1 <= MOD_ROWS

    rows = _Rows(bsz, seq, ctx_len, d)

    c8 = jnp.concatenate([c, c_ctx[None, :], jnp.zeros((MOD_ROWS - bsz - 1, d), F32)], axis=0)
    mods = _adaln_all(c8, ada_down, ada_up, ada_b)

    a_heads, a_qk, a_v = d // 256, 64, 128
    b_heads, b_qk, b_v = d // 512, 128, 256
    cb = d // 4
    tabs = jnp.stack([_rope_tables(seq, a_qk), _rope_tables(seq, b_qk)])
    attn_rows = 256
    ret_chunk = min(256, ctx_len)
    c_heads, c_qk, c_v = d // 512, 256, 512
    gla_rank = gla_w_up.shape[2]
    gla_chunk = min(128, ctx_len)

    def even_mixer(u, e, lam_init, w_in):
        p, w_out = _matmul(u, w_in, w_in.shape[1], t_all, BF16, nxt=(even_w_out, (e,)))
        tr = rows.tr
        n_lat_tiles = t_lat // tr
        per_batch = seq // tr
        ctx_tiles = ctx_len // tr
        table = pl.BlockSpec((1, 3, tr, LANES), lambda i, jo: (jnp.where(jo < 2, 0, 1), 0, i % per_batch, 0))
        qk = pl.pallas_call(
            functools.partial(_rope_kernel, n_lat_tiles=n_lat_tiles, rot=(a_qk // 4, a_qk // 4, b_qk // 4, b_qk // 4),
                              scales=(a_qk ** -0.5 * math.log2(math.e),) * 2 + (1.0, b_qk ** -0.5)),
            grid=(t_all // tr, 4),
            in_specs=[pl.BlockSpec((tr, cb), lambda i, jo: (i, jnp.where(jo < 2, jo, jo + 4))), table],
            out_specs=pl.BlockSpec((tr, cb), lambda i, jo: (i, jo)),
            out_shape=jax.ShapeDtypeStruct((t_all, 4 * cb), BF16),
            compiler_params=_cp(2, 32),
            name="rope_q",
        )(p, tabs)

        def kv_row(i):
            lat = (i // per_batch) * (per_batch + ctx_tiles) + ctx_tiles + i % per_batch
            j = i - n_lat_tiles
            return jnp.where(i < n_lat_tiles, lat, (j // ctx_tiles) * (per_batch + ctx_tiles) + j % ctx_tiles)

        kv = pl.pallas_call(
            functools.partial(_rope_kernel, n_lat_tiles=n_lat_tiles, rot=(a_qk // 4, a_qk // 4, 0, 0),
                              scales=(1.0,) * 4),
            grid=(t_all // tr, 4),
            in_specs=[pl.BlockSpec((tr, cb), lambda i, jo: (i, jo + 2)), table],
            out_specs=pl.BlockSpec((tr, cb), lambda i, jo: (kv_row(i), jo)),
            out_shape=jax.ShapeDtypeStruct((t_all, 4 * cb), BF16),
            compiler_params=_cp(2, 32),
            name="rope_kv",
        )(p, tabs)

        ctx_blk = t_lat // ctx_len
        hb = LANES
        va0 = 2 * cb // hb
        n_kv = seq + ctx_len
        lam_p = diff_lambda[e]
        dg = diff_g[e].reshape(1, a_v)
        small = [pl.BlockSpec((4, a_qk), lambda b, h: (0, 0)), pl.BlockSpec((1, a_v), lambda b, h: (0, 0))]
        r_lat, r_ctx = min(attn_rows, seq), min(attn_rows, ctx_len)
        mixed = pl.pallas_call(
            functools.partial(_attn_kernel, lam_init=lam_init, hd=a_qk, rsub=r_lat),
            grid=(bsz, a_heads),
            in_specs=[pl.BlockSpec((seq, hb), lambda b, h: (b, h)),
                      pl.BlockSpec((n_kv, hb), lambda b, h: (b, h)),
                      pl.BlockSpec((n_kv, hb), lambda b, h: (b, va0 + h))] + small,
            out_specs=pl.BlockSpec((seq, hb), lambda b, h: (b, h)),
            out_shape=jax.ShapeDtypeStruct((t_all, d), BF16),
            compiler_params=_cp(2, 48),
            name="diff_attn_lat",
        )(qk, kv, kv, lam_p, dg)
        kv_ctx = n_kv // ctx_len
        mixed = pl.pallas_call(
            functools.partial(_attn_kernel, lam_init=lam_init, hd=a_qk, rsub=r_ctx),
            grid=(bsz, a_heads),
            in_specs=[pl.BlockSpec((ctx_len, hb), lambda b, h: (ctx_blk + b, h)),
                      pl.BlockSpec((ctx_len, hb), lambda b, h: (b * kv_ctx, h)),
                      pl.BlockSpec((ctx_len, hb), lambda b, h: (b * kv_ctx, va0 + h))] + small
                     + [pl.BlockSpec(memory_space=pl.ANY)],
            out_specs=pl.BlockSpec((ctx_len, hb), lambda b, h: (ctx_blk + b, h)),
            out_shape=jax.ShapeDtypeStruct((t_all, d), BF16),
            input_output_aliases={5: 0},
            compiler_params=_cp(2, 32),
            name="diff_attn_ctx",
        )(qk, kv, kv, lam_p, dg, mixed)

        rc = ret_chunk
        hp = _pick(b_heads, (4, 2, 1))
        wq, wv = hp * b_qk, hp * b_v
        qr0, kr0 = 2 * cb // wq, 3 * cb // wq
        vr0, gr0, mo0 = 8 * cb // wv, 10 * cb // wv, 2 * cb // wv
        lgt = ret_logit[e].reshape(2 * b_heads, 1, 1)
        outs = None
        for reverse in (False, True):
            rmap, nsteps = _scan_rows(bsz, seq, ctx_len, rc, reverse)
            di = 1 if reverse else 0
            in_specs = [pl.BlockSpec((rc, wq), lambda b, h, s: (rmap(b, s), qr0 + h)),
                        pl.BlockSpec((rc, wq), lambda b, h, s: (rmap(b, s), kr0 + h)),
                        pl.BlockSpec((rc, wv), lambda b, h, s: (rmap(b, s), vr0 + h)),
                        pl.BlockSpec((hp, 1, 1), lambda b, h, s: (di * (b_heads // hp) + h, 0, 0))]
            args = [qk, qk, p, lgt]
            if reverse:
                in_specs += [pl.BlockSpec((rc, wv), lambda b, h, s: (rmap(b, s), h)),
                             pl.BlockSpec((rc, wv), lambda b, h, s: (rmap(b, s), gr0 + h)),
                             pl.BlockSpec((hp, 1, b_v), lambda b, h, s: (h, 0, 0)),
                             pl.BlockSpec(memory_space=pl.ANY)]
                args += [outs, p, ret_g[e].reshape(b_heads, 1, b_v), mixed]
                out_spec = pl.BlockSpec((rc, wv), lambda b, h, s: (rmap(b, s), mo0 + h))
                out_shape = jax.ShapeDtypeStruct((t_all, d), BF16)
                aliases = {7: 0}
            else:
                out_spec = pl.BlockSpec((rc, wv), lambda b, h, s: (rmap(b, s), h))
                out_shape = jax.ShapeDtypeStruct((t_all, b_heads * b_v), F32)
                aliases = {}
            outs = pl.pallas_call(
                functools.partial(_ret_kernel, reverse=reverse, post=reverse, chunk=rc, dk=b_qk, dv=b_v, heads=hp),
                grid=(bsz, b_heads // hp, nsteps),
                in_specs=in_specs,
                out_specs=out_spec,
                out_shape=out_shape,
                scratch_shapes=[pltpu.VMEM((hp, b_qk, b_v), F32)],
                input_output_aliases=aliases,
                compiler_params=_cp(3, 32),
                name="retention_bwd" if reverse else "retention_fwd",
            )(*args)
        return outs, w_out

    def odd_mixer(u, o, w_in):
        main_w = c_heads * (2 * c_qk + 2 * c_v)
        w_z = jnp.pad(w_in[:, main_w:], ((0, 0), (0, LANES - 2 * gla_rank)))
        p, w_out = _matmul(u, w_in, main_w, t_all, BF16, nxt=(odd_w_out, (o,)))
        z, _ = _matmul(u, w_z, LANES, t_all, F32)
        gc = gla_chunk
        hp = _pick(c_heads, (4, 2, 1))
        wq, wv = hp * c_qk, hp * c_v
        k0, v0, r0 = c_heads * c_qk // wq, 2 * c_heads * c_qk // wv, (2 * c_heads * c_qk + c_heads * c_v) // wv
        outs = None
        for reverse in (False, True):
            rmap, nsteps = _scan_rows(bsz, seq, ctx_len, gc, reverse)
            di = 1 if reverse else 0
            wup = jnp.zeros((LANES, c_heads * c_qk), F32).at[di * gla_rank:(di + 1) * gla_rank].set(gla_w_up[o, di])
            bias = gla_b[o, di].reshape(1, c_heads * c_qk)
            sel, masks = _gla_tables(gc, reverse)
            in_specs = [pl.BlockSpec((gc, wq), lambda b, h, s: (rmap(b, s), h)),
                        pl.BlockSpec((gc, wq), lambda b, h, s: (rmap(b, s), k0 + h)),
                        pl.BlockSpec((gc, wv), lambda b, h, s: (rmap(b, s), v0 + h)),
                        pl.BlockSpec((gc, LANES), lambda b, h, s: (rmap(b, s), 0)),
                        pl.BlockSpec((LANES, wq), lambda b, h, s: (0, h)),
                        pl.BlockSpec((1, wq), lambda b, h, s: (0, h)),
                        pl.BlockSpec(sel.shape, lambda b, h, s: (0, 0)),
                        pl.BlockSpec(masks.shape, lambda b, h, s: (0, 0, 0))]
            args = [p, p, p, z, wup, bias, sel, masks]
            out_spec = pl.BlockSpec((gc, wv), lambda b, h, s: (rmap(b, s), h))
            if reverse:
                in_specs += [pl.BlockSpec((gc, wv), lambda b, h, s: (rmap(b, s), h)),
                             pl.BlockSpec((gc, wv), lambda b, h, s: (rmap(b, s), r0 + h)),
                             pl.BlockSpec((hp, 1, c_v), lambda b, h, s: (h, 0, 0))]
                args += [outs, p, gla_g[o].reshape(c_heads, 1, c_v)]
                out_shape = jax.ShapeDtypeStruct((t_all, d), BF16)
            else:
                out_shape = jax.ShapeDtypeStruct((t_all, c_heads * c_v), F32)
            outs = pl.pallas_call(
                functools.partial(_gla_kernel, post=reverse, chunk=gc, dk=c_qk, dv=c_v, heads=hp,
                                  last_row=0 if reverse else gc - 1),
                grid=(bsz, c_heads // hp, nsteps),
                in_specs=in_specs,
                out_specs=out_spec,
                out_shape=out_shape,
                scratch_shapes=[pltpu.VMEM((hp, c_v, c_qk), F32)],
                compiler_params=_cp(3, 40),
                name="gla_bwd" if reverse else "gla_fwd",
            )(*args)
        return outs, w_out

    h = jnp.concatenate([x.reshape(t_lat, d), ctx.reshape(bsz * ctx_len, d)], axis=0)
    u = rows.pre(h, norm_g[0], mods[0], 0, 0)
    w_ffn = ffn_in[0, 0].astype(BF16)
    for l in range(depth):
        g, m = norm_g[l], mods[l]
        last = l == depth - 1
        z, w_ffn = _ffn_in(u, w_ffn, t_all, nxt=(ffn_out, (l, 0)))
        y, w_mix = _matmul(z, w_ffn, d, t_all, BF16, nxt=(even_w_in if l % 2 == 0 else odd_w_in, (l // 2,)))
        h, u = rows.post_pre(y, h, g, m, 1, 2, 0.5, g, m, 2, 3)
        if l % 2 == 0:
            mixed, w_mix = even_mixer(u, l // 2, 0.8 - 0.6 * math.exp(-0.3 * l), w_mix)
        else:
            mixed, w_mix = odd_mixer(u, l // 2, w_mix)
        n_rows = t_lat if last else t_all
        y, w_ffn = _matmul(mixed, w_mix, d, n_rows, BF16, nxt=(ffn_in, (l, 1)))
        h, u = rows.post_pre(y, h, g, m, 3, 5, 1.0, g, m, 4, 6)
        z, w_ffn = _ffn_in(u, w_ffn, n_rows, nxt=(ffn_out, (l, 1)))
        y, w_ffn = _matmul(z, w_ffn, d, n_rows, BF16, nxt=None if last else (ffn_in, (l + 1, 0)))
        if last:
            h = rows.post(y, h, g, m, 5, 8, 0.5)
        else:
            h, u = rows.post_pre(y, h, g, m, 5, 8, 0.5, norm_g[l + 1], mods[l + 1], 0, 0)
    return h.reshape(bsz, seq, d)
```
